```python
import jax, jax.numpy as jnp
from jax import lax
import numpy as np

D_MODEL = 1024
BATCH = 8
SEQ = 2048
DEPTH = 4
DEC_BATCH = 128
DEC_SEQ = 1
PAST_LEN = 16384
PAGE_SIZE = 128

POOL_WIDTH = D_MODEL // 2
POOL_WINDOWS = (2, 4, 8, 16)
POOL_GROUPS = len(POOL_WINDOWS)
POOL_GROUP_DIM = POOL_WIDTH // POOL_GROUPS
POOL_BUF = max(POOL_WINDOWS) - 1
RET_HEADS = 8
RET_HEAD_DIM = 64
RET_WIDTH = RET_HEADS * RET_HEAD_DIM
RET_CHUNK = 128
ROPE_THETA = 10000.0
MIX_WIDTH = POOL_WIDTH + RET_WIDTH
IN_WIDTH = POOL_WIDTH + 4 * RET_WIDTH
D_FF = 2816
N_EXPERTS = 8
TOP_K = 2
D_FF_EXPERT = 1408
N_DENSE = (DEPTH + 1) // 2
N_MOE = DEPTH // 2
PLE_DIM = 256
NORM_EPS = 1e-6

kernel_name = "hymba_pool_retention_moe_step"


def rmsnorm(x, g):
    xf = x.astype(jnp.float32)
    y = xf * lax.rsqrt(jnp.mean(xf * xf, axis=-1, keepdims=True) + NORM_EPS)
    return (y * g.astype(jnp.float32)).astype(x.dtype)


def rope(x, pos):
    half = RET_HEAD_DIM // 2
    inv = ROPE_THETA ** (-jnp.arange(half, dtype=jnp.float32) / half)
    ang = pos.astype(jnp.float32)[:, None] * inv[None, :]
    cos = jnp.cos(ang)[None, :, None, :]
    sin = jnp.sin(ang)[None, :, None, :]
    xf = x.astype(jnp.float32)
    x1, x2 = xf[..., :half], xf[..., half:]
    return jnp.concatenate([x1 * cos - x2 * sin, x1 * sin + x2 * cos], axis=-1)


def pool_mixer(u, buf, pos0, w_pool, scale):
    B, L, _ = u.shape
    xp = jnp.concatenate([buf.astype(u.dtype), u], axis=1)
    c = jnp.cumsum(xp.astype(jnp.float32), axis=1)
    c = jnp.pad(c, ((0, 0), (1, 0), (0, 0)))
    end = c[:, POOL_BUF + 1:POOL_BUF + 1 + L]
    pos = pos0 + jnp.arange(L)
    means = []
    for g, w in enumerate(POOL_WINDOWS):
        sl = slice(g * POOL_GROUP_DIM, (g + 1) * POOL_GROUP_DIM)
        s = end[:, :, sl] - c[:, POOL_BUF + 1 - w:POOL_BUF + 1 - w + L, sl]
        cnt = jnp.minimum(pos + 1, w).astype(jnp.float32)[None, :, None]
        means.append(s / cnt)
    mean = jnp.concatenate(means, axis=-1)
    d = (mean - u.astype(jnp.float32)).reshape(B, L, POOL_GROUPS, POOL_GROUP_DIM)
    y = jnp.einsum('blgc,gcd->blgd', d, w_pool.astype(jnp.float32)).reshape(B, L, POOL_WIDTH)
    y = y * scale.astype(jnp.float32)
    return y, xp[:, -POOL_BUF:]


def retention(q, k, v, S0):
    B, L, H, D = q.shape
    C = RET_CHUNK if L % RET_CHUNK == 0 else L
    n = L // C
    log_g = jnp.log1p(-jnp.power(2.0, -5.0 - jnp.arange(H, dtype=jnp.float32)))
    idx = jnp.arange(C, dtype=jnp.float32)
    rel = idx[:, None] - idx[None, :]
    dmask = jnp.where(rel >= 0, jnp.exp(log_g[:, None, None] * jnp.maximum(rel, 0.0)), 0.0)
    xi = jnp.exp(log_g[None, :] * (idx[:, None] + 1.0))
    zeta = jnp.exp(log_g[None, :] * (C - 1.0 - idx)[:, None])
    g_c = jnp.exp(log_g * C)

    def to_chunks(t):
        return t.reshape(B, n, C, H, D).swapaxes(0, 1)

    def step(S, qkv):
        qc, kc, vc = qkv
        sc = jnp.einsum('bnhd,bmhd->bhnm', qc, kc) * dmask[None]
        inner = jnp.einsum('bhnm,bmhe->bnhe', sc, vc)
        cross = jnp.einsum('bnhd,bhde->bnhe', qc, S) * xi[None, :, :, None]
        S = S * g_c[None, :, None, None] + jnp.einsum('bmhd,bmhe->bhde', kc * zeta[None, :, :, None], vc)
        return S, inner + cross

    S, o = lax.scan(step, S0.astype(jnp.float32), (to_chunks(q), to_chunks(k), to_chunks(v)))
    return o.swapaxes(0, 1).reshape(B, L, H, D), S


def mixer(hn, buf, S, pos0, w_in, w_pool, pool_scale, w_o):
    B, L, _ = hn.shape
    z = jnp.einsum('bld,de->ble', hn, w_in)
    u = z[..., :POOL_WIDTH]
    q, k, v, g = jnp.split(z[..., POOL_WIDTH:], 4, axis=-1)
    pos = pos0 + jnp.arange(L)
    q = rope(q.reshape(B, L, RET_HEADS, RET_HEAD_DIM), pos)
    k = rope(k.reshape(B, L, RET_HEADS, RET_HEAD_DIM), pos) * (RET_HEAD_DIM ** -0.5)
    v = v.reshape(B, L, RET_HEADS, RET_HEAD_DIM).astype(jnp.float32)
    o, S_new = retention(q, k, v, S)
    mu = jnp.mean(o, axis=-1, keepdims=True)
    var = jnp.mean(jnp.square(o - mu), axis=-1, keepdims=True)
    o = ((o - mu) * lax.rsqrt(var + NORM_EPS)).reshape(B, L, RET_WIDTH)
    o = o * jax.nn.silu(g.astype(jnp.float32))
    pool_out, buf_new = pool_mixer(u, buf, pos0, w_pool, pool_scale)
    mix = jnp.concatenate([pool_out, o], axis=-1).astype(hn.dtype)
    return jnp.einsum('ble,ed->bld', mix, w_o), buf_new, S_new.astype(S.dtype)


def swiglu(x, wg, wu, wd):
    h = jax.nn.silu(jnp.einsum('bld,df->blf', x, wg)) * jnp.einsum('bld,df->blf', x, wu)
    return jnp.einsum('blf,fd->bld', h, wd)


def moe(x, w_router, wg, wu, wd):
    logits = jnp.einsum('bld,de->ble', x.astype(jnp.float32), w_router.astype(jnp.float32))
    vals, idx = lax.top_k(logits, TOP_K)
    wts = jax.nn.softmax(vals, axis=-1)
    gates = jnp.sum(jax.nn.one_hot(idx, N_EXPERTS, dtype=jnp.float32) * wts[..., None], axis=-2)
    y = jnp.zeros(x.shape, jnp.float32)
    for e in range(N_EXPERTS):
        y = y + gates[..., e:e + 1] * swiglu(x, wg[e], wu[e], wd[e]).astype(jnp.float32)
    return y.astype(x.dtype)


def layer(i, h, p_l, buf, S, pos0, norm1, w_in, w_pool, pool_scale, w_o, norm2,
          ffn_w_gate, ffn_w_up, ffn_w_down, moe_router, moe_w_gate, moe_w_up, moe_w_down,
          w_ple, w_ple_gate):
    mix, buf_new, S_new = mixer(rmsnorm(h, norm1[i]), buf, S, pos0, w_in[i], w_pool[i], pool_scale[i], w_o[i])
    h = h + mix
    hn = rmsnorm(h, norm2[i])
    j = i // 2
    if i % 2 == 0:
        f = swiglu(hn, ffn_w_gate[j], ffn_w_up[j], ffn_w_down[j])
    else:
        f = moe(hn, moe_router[j], moe_w_gate[j], moe_w_up[j], moe_w_down[j])
    h = h + f
    gate = jax.nn.sigmoid(jnp.einsum('bld,de->ble', h, w_ple_gate[i]).astype(jnp.float32))
    e = jnp.einsum('blp,pd->bld', p_l, w_ple[i]).astype(jnp.float32)
    h = h + (gate * e).astype(h.dtype)
    return h, buf_new, S_new


def setup_inputs(seed: int = 0) -> dict:
    key = jax.random.key(seed)
    ks = jax.random.split(key, 24)
    f32 = jnp.float32
    nrm = lambda k, s, sc: jax.random.normal(k, s, f32) * sc
    return {
        "x_prompt": nrm(ks[0], (BATCH, SEQ, D_MODEL), 1.0),
        "x_sample": nrm(ks[1], (DEC_BATCH, DEC_SEQ, D_MODEL), 1.0),
        "state_pool": nrm(ks[2], (DEPTH, DEC_BATCH, POOL_BUF, POOL_WIDTH), 1.0),
        "state_ret": nrm(ks[3], (DEPTH, DEC_BATCH, RET_HEADS, RET_HEAD_DIM, RET_HEAD_DIM), 0.5),
        "p_prompt": nrm(ks[4], (DEPTH, BATCH, SEQ, PLE_DIM), 1.0),
        "p_sample": nrm(ks[5], (DEPTH, DEC_BATCH, DEC_SEQ, PLE_DIM), 1.0),
        "norm1": 1.0 + nrm(ks[6], (DEPTH, D_MODEL), 0.05),
        "w_in": nrm(ks[7], (DEPTH, D_MODEL, IN_WIDTH), D_MODEL ** -0.5),
        "w_pool": nrm(ks[8], (DEPTH, POOL_GROUPS, POOL_GROUP_DIM, POOL_GROUP_DIM), POOL_GROUP_DIM ** -0.5),
        "pool_scale": 1.0 + nrm(ks[9], (DEPTH, POOL_WIDTH), 0.1),
        "w_o": nrm(ks[10], (DEPTH, MIX_WIDTH, D_MODEL), MIX_WIDTH ** -0.5),
        "norm2": 1.0 + nrm(ks[11], (DEPTH, D_MODEL), 0.05),
        "ffn_w_gate": nrm(ks[12], (N_DENSE, D_MODEL, D_FF), D_MODEL ** -0.5),
        "ffn_w_up": nrm(ks[13], (N_DENSE, D_MODEL, D_FF), D_MODEL ** -0.5),
        "ffn_w_down": nrm(ks[14], (N_DENSE, D_FF, D_MODEL), D_FF ** -0.5),
        "moe_router": nrm(ks[15], (N_MOE, D_MODEL, N_EXPERTS), D_MODEL ** -0.5),
        "moe_w_gate": nrm(ks[16], (N_MOE, N_EXPERTS, D_MODEL, D_FF_EXPERT), D_MODEL ** -0.5),
        "moe_w_up": nrm(ks[17], (N_MOE, N_EXPERTS, D_MODEL, D_FF_EXPERT), D_MODEL ** -0.5),
        "moe_w_down": nrm(ks[18], (N_MOE, N_EXPERTS, D_FF_EXPERT, D_MODEL), D_FF_EXPERT ** -0.5),
        "w_ple": nrm(ks[19], (DEPTH, PLE_DIM, D_MODEL), PLE_DIM ** -0.5),
        "w_ple_gate": nrm(ks[20], (DEPTH, D_MODEL, D_MODEL), D_MODEL ** -0.5),
        "final_norm": 1.0 + nrm(ks[21], (D_MODEL,), 0.05),
    }


def reference(x_prompt, x_sample, state_pool, state_ret, p_prompt, p_sample, norm1, w_in, w_pool,
              pool_scale, w_o, norm2, ffn_w_gate, ffn_w_up, ffn_w_down, moe_router, moe_w_gate,
              moe_w_up, moe_w_down, w_ple, w_ple_gate, final_norm):
    hp, hs = x_prompt, x_sample
    Bp = x_prompt.shape[0]
    pool_p, ret_p, pool_s, ret_s = [], [], [], []
    for i in range(DEPTH):
        buf0 = jnp.zeros((Bp, POOL_BUF, POOL_WIDTH), x_prompt.dtype)
        S0 = jnp.zeros((Bp, RET_HEADS, RET_HEAD_DIM, RET_HEAD_DIM), x_prompt.dtype)
        hp, bp, sp = layer(i, hp, p_prompt[i], buf0, S0, 0, norm1, w_in, w_pool, pool_scale, w_o, norm2,
                           ffn_w_gate, ffn_w_up, ffn_w_down, moe_router, moe_w_gate, moe_w_up, moe_w_down,
                           w_ple, w_ple_gate)
        hs, bs, ss = layer(i, hs, p_sample[i], state_pool[i], state_ret[i], PAST_LEN, norm1, w_in, w_pool,
                           pool_scale, w_o, norm2, ffn_w_gate, ffn_w_up, ffn_w_down, moe_router, moe_w_gate,
                           moe_w_up, moe_w_down, w_ple, w_ple_gate)
        pool_p.append(bp); ret_p.append(sp); pool_s.append(bs); ret_s.append(ss)
    y_prompt = rmsnorm(hp, final_norm)
    y_sample = rmsnorm(hs, final_norm)
    return (y_prompt, y_sample, jnp.stack(pool_p), jnp.stack(ret_p), jnp.stack(pool_s), jnp.stack(ret_s))
```

```python
import functools
import math

import jax
import jax.numpy as jnp
from jax import lax
from jax.experimental import pallas as pl
from jax.experimental.pallas import tpu as pltpu

F32 = jnp.float32
BF16 = jnp.bfloat16

D_MODEL = 1024
DEPTH = 4
PAST_LEN = 16384
POOL_WIDTH = 512
POOL_WINDOWS = (2, 4, 8, 16)
POOL_GROUP_DIM = 128
POOL_BUF = 15
RET_HEADS = 8
RET_HEAD_DIM = 64
RET_WIDTH = 512
ROPE_THETA = 10000.0
IN_WIDTH = POOL_WIDTH + 4 * RET_WIDTH
SECTION = 512
D_FF_TILE = 1408
N_EXPERTS = 8
PLE_DIM = 256
NORM_EPS = 1e-6
LANES = 128
TAIL_ROWS = 16
VMEM_LIMIT = 56 * 1024 * 1024


def _rms(x, g):
    return x * lax.rsqrt(jnp.mean(x * x, axis=-1, keepdims=True) + NORM_EPS) * g


def _dot(a, b):
    return jnp.dot(a, b, preferred_element_type=F32)


def _dot_nt(a, b):
    return lax.dot_general(a, b, (((1,), (1,)), ((), ())), preferred_element_type=F32)


def _dot_tn(a, b):
    return lax.dot_general(a, b, (((0,), (0,)), ((), ())), preferred_element_type=F32)


def _params(*sem):
    return pltpu.CompilerParams(dimension_semantics=sem, vmem_limit_bytes=VMEM_LIMIT)


def _rope_slab(x, cos, sin, first_half):
    fwd = pltpu.roll(x, 32, 1)
    bwd = pltpu.roll(x, LANES - 32, 1)
    return x * cos + jnp.where(first_half, bwd, fwd) * sin


def _in_proj_kernel(h_ref, n1_ref, w_ref, cos_ref, sin_ref, z_ref):
    hn = _rms(h_ref[...], n1_ref[...]).astype(BF16)
    cos = cos_ref[...]
    sin = sin_ref[...]
    lane = lax.broadcasted_iota(jnp.int32, (hn.shape[0], LANES), 1)
    first_half = (lane & 32) == 0
    for s in range(IN_WIDTH // SECTION):
        zs = _dot(hn, w_ref[:, s * SECTION:(s + 1) * SECTION])
        if s in (1, 2):
            scale = 1.0 if s == 1 else RET_HEAD_DIM ** -0.5
            for c in range(SECTION // LANES):
                slab = _rope_slab(zs[:, c * LANES:(c + 1) * LANES], cos, sin, first_half)
                z_ref[:, s * SECTION + c * LANES:s * SECTION + (c + 1) * LANES] = slab * scale
        else:
            z_ref[:, s * SECTION:(s + 1) * SECTION] = zs


def _in_proj(h, n1, w_in, cos_t, sin_t, tm):
    T = h.shape[0]
    n_tab = cos_t.shape[0] // tm if cos_t.shape[0] > 1 else 1
    tb = tm if cos_t.shape[0] > 1 else 1
    return pl.pallas_call(
        _in_proj_kernel,
        grid=(T // tm,),
        in_specs=[
            pl.BlockSpec((tm, D_MODEL), lambda i: (i, 0)),
            pl.BlockSpec((1, D_MODEL), lambda i: (0, 0)),
            pl.BlockSpec((D_MODEL, IN_WIDTH), lambda i: (0, 0)),
            pl.BlockSpec((tb, LANES), lambda i: (i % n_tab, 0)),
            pl.BlockSpec((tb, LANES), lambda i: (i % n_tab, 0)),
        ],
        out_specs=pl.BlockSpec((tm, IN_WIDTH), lambda i: (i, 0)),
        out_shape=jax.ShapeDtypeStruct((T, IN_WIDTH), F32),
        compiler_params=_params("arbitrary"),
        name="in_proj",
    )(h, n1, w_in, cos_t, sin_t)


def _route(x, rhi_ref, rlo_ref):
    x_hi = x.astype(BF16)
    x_lo = (x - x_hi.astype(F32)).astype(BF16)
    logits = _dot(x_hi, rhi_ref[...]) + _dot(x_lo, rhi_ref[...]) + _dot(x_hi, rlo_ref[...])
    lane = lax.broadcasted_iota(jnp.int32, logits.shape, 1)
    neg = jnp.float32(-jnp.inf)
    lg = jnp.where(lane < N_EXPERTS, logits, neg)
    m1 = jnp.max(lg, axis=-1, keepdims=True)
    i1 = jnp.min(jnp.where(lg == m1, lane, LANES), axis=-1, keepdims=True)
    lg2 = jnp.where(lane == i1, neg, lg)
    m2 = jnp.max(lg2, axis=-1, keepdims=True)
    i2 = jnp.min(jnp.where(lg2 == m2, lane, LANES), axis=-1, keepdims=True)
    e2 = jnp.exp(m2 - m1)
    den = 1.0 + e2
    return jnp.where(lane == i1, 1.0 / den, 0.0) + jnp.where(lane == i2, e2 / den, 0.0)


def _mixer_tail(mix, h_ref, wo_ref, n2_ref, router_refs, h1_ref, hn2_ref, gates_ref):
    h1 = h_ref[...] + _dot(mix.astype(BF16), wo_ref[...])
    h1_ref[...] = h1
    hn2 = _rms(h1, n2_ref[...])
    hn2_ref[...] = hn2.astype(BF16)
    if router_refs:
        gates_ref[...] = _route(hn2, *router_refs)


def _pool_project(d_slabs, u, wpool_ref, pscale_ref):
    ys = []
    for g in range(len(POOL_WINDOWS)):
        sl = slice(g * POOL_GROUP_DIM, (g + 1) * POOL_GROUP_DIM)
        ys.append(_dot((d_slabs[g] - u[:, sl]).astype(BF16), wpool_ref[g]) * pscale_ref[:, sl])
    return ys


def _mixer_prompt_kernel(*refs, cb, decay_c, moe):
    (u_ref, q_ref, k_ref, v_ref, g_ref, h_ref, dmask_ref, xi_ref, zeta_ref,
     wpool_ref, pscale_ref, wo_ref, n2_ref) = refs[:13]
    n_in = 15 if moe else 13
    router_refs = refs[13:n_in]
    h1_ref, hn2_ref = refs[n_in:n_in + 2]
    gates_ref = refs[n_in + 2] if moe else None
    pool_out_ref, ret_out_ref, s_scr, p_scr = refs[-4:]

    c = pl.program_id(1)

    @pl.when(c == 0)
    def _():
        s_scr[...] = jnp.zeros_like(s_scr)
        p_scr[0:TAIL_ROWS, :] = jnp.zeros((TAIL_ROWS, POOL_WIDTH), F32)

    u = u_ref[...]
    p_scr[TAIL_ROWS:TAIL_ROWS + cb, :] = u

    pos = c * cb + lax.broadcasted_iota(jnp.int32, (cb, POOL_GROUP_DIM), 0)
    means = []
    for g, w in enumerate(POOL_WINDOWS):
        sl = slice(g * POOL_GROUP_DIM, (g + 1) * POOL_GROUP_DIM)
        s = u[:, sl]
        for j in range(1, w):
            s = s + p_scr[TAIL_ROWS - j:TAIL_ROWS - j + cb, sl]
        means.append(s / jnp.minimum(pos + 1, w).astype(F32))
    ys = _pool_project(means, u, wpool_ref, pscale_ref)
    pool_out_ref[0] = p_scr[cb + TAIL_ROWS - POOL_BUF:cb + TAIL_ROWS, :]
    p_scr[0:TAIL_ROWS, :] = p_scr[cb:cb + TAIL_ROWS, :]

    q = q_ref[...]
    k = k_ref[...]
    qb = q.astype(BF16)
    kb = k.astype(BF16)
    kz = (k * zeta_ref[...]).astype(BF16)
    vb = v_ref[...].astype(BF16)
    xi = xi_ref[...]
    os_ = []
    for hd in range(RET_HEADS):
        sl = slice(hd * RET_HEAD_DIM, (hd + 1) * RET_HEAD_DIM)
        qh, kh, vh = qb[:, sl], kb[:, sl], vb[:, sl]
        sc = _dot_nt(qh, kh) * dmask_ref[hd]
        s_prev = s_scr[hd]
        o_h = _dot(sc.astype(BF16), vh) + _dot(qh, s_prev.astype(BF16)) * xi[:, sl]
        s_scr[hd] = s_prev * decay_c[hd] + _dot_tn(kz[:, sl], vh)
        mu = jnp.mean(o_h, axis=-1, keepdims=True)
        var = jnp.mean(jnp.square(o_h - mu), axis=-1, keepdims=True)
        os_.append((o_h - mu) * lax.rsqrt(var + NORM_EPS))
    ret_out_ref[0] = s_scr[...]
    o = jnp.concatenate(os_, axis=1) * jax.nn.silu(g_ref[...])
    mix = jnp.concatenate(ys + [o], axis=1)
    _mixer_tail(mix, h_ref, wo_ref, n2_ref, router_refs, h1_ref, hn2_ref, gates_ref)


def _mixer_prompt(z, h, tabs, wpool, pscale, wo, n2, router, B, L, cb):
    T = B * L
    nc = L // cb
    moe = router is not None
    row = lambda b, c: b * nc + c
    zspec = lambda s: pl.BlockSpec((cb, SECTION), lambda b, c, s=s: (row(b, c), s))
    const2 = lambda shape: pl.BlockSpec(shape, lambda b, c: (0, 0))
    const3 = lambda shape: pl.BlockSpec(shape, lambda b, c: (0, 0, 0))
    in_specs = [zspec(0), zspec(1), zspec(2), zspec(3), zspec(4),
                pl.BlockSpec((cb, D_MODEL), lambda b, c: (row(b, c), 0)),
                const3((RET_HEADS, cb, cb)), const2((cb, RET_WIDTH)), const2((cb, RET_WIDTH)),
                const3((len(POOL_WINDOWS), POOL_GROUP_DIM, POOL_GROUP_DIM)), const2((1, POOL_WIDTH)),
                const2((D_MODEL, D_MODEL)), const2((1, D_MODEL))]
    args = [z, z, z, z, z, h, tabs["dmask"], tabs["xi"], tabs["zeta"], wpool, pscale, wo, n2]
    out_specs = [pl.BlockSpec((cb, D_MODEL), lambda b, c: (row(b, c), 0)),
                 pl.BlockSpec((cb, D_MODEL), lambda b, c: (row(b, c), 0))]
    out_shape = [jax.ShapeDtypeStruct((T, D_MODEL), F32), jax.ShapeDtypeStruct((T, D_MODEL), BF16)]
    if moe:
        in_specs += [const2((D_MODEL, LANES)), const2((D_MODEL, LANES))]
        args += list(router)
        out_specs.append(pl.BlockSpec((cb, LANES), lambda b, c: (row(b, c), 0)))
        out_shape.append(jax.ShapeDtypeStruct((T, LANES), F32))
    out_specs += [pl.BlockSpec((1, POOL_BUF, POOL_WIDTH), lambda b, c: (b, 0, 0)),
                  pl.BlockSpec((1, RET_HEADS, RET_HEAD_DIM, RET_HEAD_DIM), lambda b, c: (b, 0, 0, 0))]
    out_shape += [jax.ShapeDtypeStruct((B, POOL_BUF, POOL_WIDTH), F32),
                  jax.ShapeDtypeStruct((B, RET_HEADS, RET_HEAD_DIM, RET_HEAD_DIM), F32)]
    outs = pl.pallas_call(
        functools.partial(_mixer_prompt_kernel, cb=cb, decay_c=tabs["decay_c"], moe=moe),
        grid=(B, nc),
        in_specs=in_specs,
        out_specs=out_specs,
        out_shape=out_shape,
        scratch_shapes=[pltpu.VMEM((RET_HEADS, RET_HEAD_DIM, RET_HEAD_DIM), F32),
                        pltpu.VMEM((TAIL_ROWS + cb, POOL_WIDTH), F32)],
        compiler_params=_params("arbitrary", "arbitrary"),
        name="mixer_prompt",
    )(*args)
    if moe:
        h1, hn2, gates, pool, ret = outs
    else:
        (h1, hn2, pool, ret), gates = outs, None
    return h1, hn2, gates, pool, ret


def _column(row, eye):
    return jnp.sum(jnp.where(eye, jnp.broadcast_to(row, eye.shape), 0.0), axis=1, keepdims=True)


def _mixer_sample_kernel(*refs, bb, decay_1, moe):
    (u_ref, q_ref, k_ref, v_ref, g_ref, h_ref, buf_ref, s_ref,
     wpool_ref, pscale_ref, wo_ref, n2_ref) = refs[:12]
    n_in = 14 if moe else 12
    router_refs = refs[12:n_in]
    h1_ref, hn2_ref = refs[n_in:n_in + 2]
    gates_ref = refs[n_in + 2] if moe else None
    buf_out_ref, s_out_ref, o_scr = refs[-3:]

    u = u_ref[...]
    means = []
    for g, w in enumerate(POOL_WINDOWS):
        s = u[:, g * POOL_GROUP_DIM:(g + 1) * POOL_GROUP_DIM]
        for j in range(1, w):
            lo = (POOL_BUF - j) * POOL_WIDTH + g * POOL_GROUP_DIM
            s = s + buf_ref[:, lo:lo + POOL_GROUP_DIM]
        means.append(s / float(min(PAST_LEN + 1, w)))
    ys = _pool_project(means, u, wpool_ref, pscale_ref)
    keep = (POOL_BUF - 1) * POOL_WIDTH
    buf_out_ref[:, 0:keep] = buf_ref[:, POOL_WIDTH:POOL_WIDTH + keep]
    buf_out_ref[:, keep:keep + POOL_WIDTH] = u

    dh = RET_HEAD_DIM
    eye = (lax.broadcasted_iota(jnp.int32, (dh, dh), 0) == lax.broadcasted_iota(jnp.int32, (dh, dh), 1))

    def per_seq(b, carry):
        qr = q_ref[pl.ds(b, 1), :]
        kr = k_ref[pl.ds(b, 1), :]
        vr = v_ref[pl.ds(b, 1), :]
        heads = []
        for hd in range(RET_HEADS):
            sl = slice(hd * dh, (hd + 1) * dh)
            qh, kh, vh = qr[:, sl], kr[:, sl], vr[:, sl]
            s_prev = s_ref[b, hd]
            cross = jnp.sum(s_prev * _column(qh, eye), axis=0, keepdims=True)
            o_h = jnp.sum(qh * kh, axis=-1, keepdims=True) * vh + cross * decay_1[hd]
            s_out_ref[b, hd] = s_prev * decay_1[hd] + _column(kh, eye) * vh
            mu = jnp.mean(o_h, axis=-1, keepdims=True)
            var = jnp.mean(jnp.square(o_h - mu), axis=-1, keepdims=True)
            heads.append((o_h - mu) * lax.rsqrt(var + NORM_EPS))
        o_scr[pl.ds(b, 1), :] = jnp.concatenate(heads, axis=1)
        return carry

    lax.fori_loop(0, bb, per_seq, 0)
    o = o_scr[...] * jax.nn.silu(g_ref[...])
    mix = jnp.concatenate(ys + [o], axis=1)
    _mixer_tail(mix, h_ref, wo_ref, n2_ref, router_refs, h1_ref, hn2_ref, gates_ref)


def _mixer_sample(z, h, buf, s0, decay_1, wpool, pscale, wo, n2, router, bb):
    B = h.shape[0]
    moe = router is not None
    flat = POOL_BUF * POOL_WIDTH
    zspec = lambda s: pl.BlockSpec((bb, SECTION), lambda i, s=s: (i, s))
    const2 = lambda shape: pl.BlockSpec(shape, lambda i: (0, 0))
    state = pl.BlockSpec((bb, RET_HEADS, RET_HEAD_DIM, RET_HEAD_DIM), lambda i: (i, 0, 0, 0))
    in_specs = [zspec(0), zspec(1), zspec(2), zspec(3), zspec(4),
                pl.BlockSpec((bb, D_MODEL), lambda i: (i, 0)),
                pl.BlockSpec((bb, flat), lambda i: (i, 0)), state,
                pl.BlockSpec((len(POOL_WINDOWS), POOL_GROUP_DIM, POOL_GROUP_DIM), lambda i: (0, 0, 0)),
                const2((1, POOL_WIDTH)), const2((D_MODEL, D_MODEL)), const2((1, D_MODEL))]
    args = [z, z, z, z, z, h, buf, s0, wpool, pscale, wo, n2]
    out_specs = [pl.BlockSpec((bb, D_MODEL), lambda i: (i, 0)), pl.BlockSpec((bb, D_MODEL), lambda i: (i, 0))]
    out_shape = [jax.ShapeDtypeStruct((B, D_MODEL), F32), jax.ShapeDtypeStruct((B, D_MODEL), BF16)]
    if moe:
        in_specs += [const2((D_MODEL, LANES)), const2((D_MODEL, LANES))]
        args += list(router)
        out_specs.append(pl.BlockSpec((bb, LANES), lambda i: (i, 0)))
        out_shape.append(jax.ShapeDtypeStruct((B, LANES), F32))
    out_specs += [pl.BlockSpec((bb, flat), lambda i: (i, 0)), state]
    out_shape += [jax.ShapeDtypeStruct((B, flat), F32), jax.ShapeDtypeStruct(s0.shape, F32)]
    outs = pl.pallas_call(
        functools.partial(_mixer_sample_kernel, bb=bb, decay_1=decay_1, moe=moe),
        grid=(B // bb,),
        in_specs=in_specs,
        out_specs=out_specs,
        out_shape=out_shape,
        scratch_shapes=[pltpu.VMEM((bb, RET_WIDTH), F32)],
        compiler_params=_params("arbitrary"),
        name="mixer_sample",
    )(*args)
    if moe:
        h1, hn2, gates, pool, ret = outs
    else:
        (h1, hn2, pool, ret), gates = outs, None
    return h1, hn2, gates, pool, ret


def _ffn_kernel(*refs, moe):
    if moe:
        x_ref, h_ref, gates_ref, wg_ref, wu_ref, wd_ref, out_ref, acc_ref = refs
    else:
        x_ref, h_ref, wg_ref, wu_ref, wd_ref, out_ref, acc_ref = refs
    f = pl.program_id(1)
    x = x_ref[...]
    hidden = jax.nn.silu(_dot(x, wg_ref[...])) * _dot(x, wu_ref[...])
    y = _dot(hidden.astype(BF16), wd_ref[...])
    if moe:
        gates = gates_ref[...]
        lane = lax.broadcasted_iota(jnp.int32, gates.shape, 1)
        y = y * jnp.sum(jnp.where(lane == f, gates, 0.0), axis=-1, keepdims=True)

    @pl.when(f == 0)
    def _():
        acc_ref[...] = y

    @pl.when(f > 0)
    def _():
        acc_ref[...] += y

    @pl.when(f == pl.num_programs(1) - 1)
    def _():
        out_ref[...] = h_ref[...] + acc_ref[...]


def _ffn(x, h, gates, wg, wu, wd, tm):
    T = x.shape[0]
    moe = gates is not None
    tf = D_FF_TILE
    row = pl.BlockSpec((tm, D_MODEL), lambda i, f: (i, 0))
    in_specs = [row, row]
    args = [x, h]
    if moe:
        nf = wg.shape[0]
        in_specs.append(pl.BlockSpec((tm, LANES), lambda i, f: (i, 0)))
        args.append(gates)
        in_specs += [pl.BlockSpec((None, D_MODEL, tf), lambda i, f: (f, 0, 0)),
                     pl.BlockSpec((None, D_MODEL, tf), lambda i, f: (f, 0, 0)),
                     pl.BlockSpec((None, tf, D_MODEL), lambda i, f: (f, 0, 0))]
    else:
        nf = wg.shape[1] // tf
        in_specs += [pl.BlockSpec((D_MODEL, tf), lambda i, f: (0, f)),
                     pl.BlockSpec((D_MODEL, tf), lambda i, f: (0, f)),
                     pl.BlockSpec((tf, D_MODEL), lambda i, f: (f, 0))]
    args += [wg, wu, wd]
    return pl.pallas_call(
        functools.partial(_ffn_kernel, moe=moe),
        grid=(T // tm, nf),
        in_specs=in_specs,
        out_specs=row,
        out_shape=jax.ShapeDtypeStruct((T, D_MODEL), F32),
        scratch_shapes=[pltpu.VMEM((tm, D_MODEL), F32)],
        compiler_params=_params("arbitrary", "arbitrary"),
        name="ffn_moe" if moe else "ffn_dense",
    )(*args)


def _ple_kernel(*refs, final):
    if final:
        h_ref, p_ref, wg_ref, wp_ref, fn_ref, out_ref = refs
    else:
        h_ref, p_ref, wg_ref, wp_ref, out_ref = refs
    h = h_ref[...]
    gate = jax.nn.sigmoid(_dot(h.astype(BF16), wg_ref[...]))
    e = _dot(p_ref[...].astype(BF16), wp_ref[...])
    h3 = h + gate * e
    out_ref[...] = _rms(h3, fn_ref[...]) if final else h3


def _ple(h, p, wg, wp, fn, tm):
    T = h.shape[0]
    final = fn is not None
    in_specs = [pl.BlockSpec((tm, D_MODEL), lambda i: (i, 0)),
                pl.BlockSpec((tm, PLE_DIM), lambda i: (i, 0)),
                pl.BlockSpec((D_MODEL, D_MODEL), lambda i: (0, 0)),
                pl.BlockSpec((PLE_DIM, D_MODEL), lambda i: (0, 0))]
    args = [h, p, wg, wp]
    if final:
        in_specs.append(pl.BlockSpec((1, D_MODEL), lambda i: (0, 0)))
        args.append(fn)
    return pl.pallas_call(
        functools.partial(_ple_kernel, final=final),
        grid=(T // tm,),
        in_specs=in_specs,
        out_specs=pl.BlockSpec((tm, D_MODEL), lambda i: (i, 0)),
        out_shape=jax.ShapeDtypeStruct((T, D_MODEL), F32),
        compiler_params=_params("arbitrary"),
        name="ple_final" if final else "ple",
    )(*args)


def _rope_tables(pos):
    half = RET_HEAD_DIM // 2
    inv = ROPE_THETA ** (-jnp.arange(half, dtype=F32) / half)
    ang = pos.astype(F32)[:, None] * inv[None, :]
    cos, sin = jnp.cos(ang), jnp.sin(ang)
    reps = LANES // RET_HEAD_DIM
    return (jnp.tile(jnp.concatenate([cos, cos], axis=-1), (1, reps)),
            jnp.tile(jnp.concatenate([-sin, sin], axis=-1), (1, reps)))


def _log_decay():
    return jnp.log1p(-jnp.power(2.0, -5.0 - jnp.arange(RET_HEADS, dtype=F32)))


def _retention_tables(C):
    log_g = _log_decay()
    idx = jnp.arange(C, dtype=F32)
    rel = idx[:, None] - idx[None, :]
    dmask = jnp.where(rel >= 0, jnp.exp(log_g[:, None, None] * jnp.maximum(rel, 0.0)), 0.0)
    xi = jnp.exp(log_g[None, :] * (idx[:, None] + 1.0))
    zeta = jnp.exp(log_g[None, :] * (C - 1.0 - idx)[:, None])
    spread = lambda t: jnp.repeat(t, RET_HEAD_DIM, axis=1)
    return {"dmask": dmask, "xi": spread(xi), "zeta": spread(zeta), "decay_c": _chunk_decay(C)}


def _chunk_decay(C):
    return tuple(math.exp(math.log1p(-(2.0 ** (-5 - hd))) * C) for hd in range(RET_HEADS))


def _choose_tile(T, cap):
    tm = min(T, cap)
    while T % tm:
        tm //= 2
    return tm


def _layer(i, h, p, W, rope, state, B, L, tabs, tm_cap=512, cb_cap=256):
    T = B * L
    tm = _choose_tile(T, tm_cap)
    router = W["router"][i // 2] if i % 2 else None
    z = _in_proj(h, W["norm1"][i], W["w_in"][i], rope[0], rope[1], _choose_tile(L, tm) if L > 1 else tm)
    if state is None:
        cb = _choose_tile(L, cb_cap)
        h1, hn2, gates, pool, ret = _mixer_prompt(
            z, h, tabs, W["w_pool"][i], W["pool_scale"][i], W["w_o"][i], W["norm2"][i], router, B, L, cb)
    else:
        h1, hn2, gates, pool, ret = _mixer_sample(
            z, h, state[0], state[1], tabs, W["w_pool"][i], W["pool_scale"][i], W["w_o"][i], W["norm2"][i],
            router, _choose_tile(B, 16))
    j = i // 2
    if i % 2 == 0:
        h2 = _ffn(hn2, h1, None, W["ffn_g"][j], W["ffn_u"][j], W["ffn_d"][j], tm)
    else:
        h2 = _ffn(hn2, h1, gates, W["moe_g"][j], W["moe_u"][j], W["moe_d"][j], tm)
    fn = W["final_norm"] if i == DEPTH - 1 else None
    return _ple(h2, p, W["w_ple_gate"][i], W["w_ple"][i], fn, tm), pool, ret


def _prepare_weights(norm1, w_in, w_pool, pool_scale, w_o, norm2, ffn_w_gate, ffn_w_up, ffn_w_down,
                     moe_router, moe_w_gate, moe_w_up, moe_w_down, w_ple, w_ple_gate, final_norm):
    r = jnp.pad(moe_router, ((0, 0), (0, 0), (0, LANES - N_EXPERTS)))
    r_hi = r.astype(BF16)
    r_lo = (r - r_hi.astype(F32)).astype(BF16)
    return {
        "norm1": norm1[:, None, :], "norm2": norm2[:, None, :], "final_norm": final_norm[None, :],
        "pool_scale": pool_scale[:, None, :],
        "w_in": w_in.astype(BF16), "w_pool": w_pool.astype(BF16), "w_o": w_o.astype(BF16),
        "ffn_g": ffn_w_gate.astype(BF16), "ffn_u": ffn_w_up.astype(BF16), "ffn_d": ffn_w_down.astype(BF16),
        "moe_g": moe_w_gate.astype(BF16), "moe_u": moe_w_up.astype(BF16), "moe_d": moe_w_down.astype(BF16),
        "router": [(r_hi[j], r_lo[j]) for j in range(r.shape[0])],
        "w_ple": w_ple.astype(BF16), "w_ple_gate": w_ple_gate.astype(BF16),
    }


def _trunk(x_prompt, x_sample, state_pool, state_ret, p_prompt, p_sample, W):
    B, L, _ = x_prompt.shape
    Bs = x_sample.shape[0]
    depth = state_pool.shape[0]
    hp = x_prompt.reshape(B * L, D_MODEL)
    hs = x_sample.reshape(Bs, D_MODEL)
    rope_p = _rope_tables(jnp.arange(L))
    rope_s = _rope_tables(PAST_LEN + jnp.arange(1))
    tabs = _retention_tables(_choose_tile(L, 256))
    decay_1 = _chunk_decay(1)
    pool_p, ret_p, pool_s, ret_s = [], [], [], []
    for i in range(depth):
        hp, bp, sp = _layer(i, hp, p_prompt[i].reshape(B * L, PLE_DIM), W, rope_p, None, B, L, tabs)
        hs, bs, ss = _layer(i, hs, p_sample[i].reshape(Bs, PLE_DIM), W, rope_s,
                            (state_pool[i].reshape(Bs, POOL_BUF * POOL_WIDTH), state_ret[i]), Bs, 1, decay_1)
        pool_p.append(bp)
        ret_p.append(sp)
        pool_s.append(bs.reshape(Bs, POOL_BUF, POOL_WIDTH))
        ret_s.append(ss)
    return (hp.reshape(B, L, D_MODEL), hs.reshape(Bs, 1, D_MODEL),
            jnp.stack(pool_p), jnp.stack(ret_p), jnp.stack(pool_s), jnp.stack(ret_s))


def kernel(x_prompt, x_sample, state_pool, state_ret, p_prompt, p_sample, norm1, w_in, w_pool, pool_scale, w_o, norm2, ffn_w_gate, ffn_w_up, ffn_w_down, moe_router, moe_w_gate, moe_w_up, moe_w_down, w_ple, w_ple_gate, final_norm):
    W = _prepare_weights(norm1, w_in, w_pool, pool_scale, w_o, norm2, ffn_w_gate, ffn_w_up, ffn_w_down,
                         moe_router, moe_w_gate, moe_w_up, moe_w_down, w_ple, w_ple_gate, final_norm)
    return _trunk(x_prompt, x_sample, state_pool, state_ret, p_prompt, p_sample, W)
```

```python
import functools
import math

import numpy as np

import jax
import jax.numpy as jnp
from jax import lax
from jax.experimental import pallas as pl
from jax.experimental.pallas import tpu as pltpu

F32 = jnp.float32
BF16 = jnp.bfloat16
I32 = jnp.int32

D_MODEL = 1024
DEPTH = 4
PAST_LEN = 16384
POOL_WIDTH = 512
POOL_WINDOWS = (2, 4, 8, 16)
POOL_GROUP_DIM = 128
POOL_BUF = 15
RET_HEADS = 8
RET_HEAD_DIM = 64
RET_WIDTH = 512
ROPE_THETA = 10000.0
IN_WIDTH = POOL_WIDTH + 4 * RET_WIDTH
SECTION = 512
D_FF_TILE = 1408
N_EXPERTS = 8
TOP_K = 2
PLE_DIM = 256
NORM_EPS = 1e-6
LANES = 128
SUBLANES = 8
TOKEN_TILES = D_MODEL // LANES
assert TOKEN_TILES == SUBLANES
TAIL_ROWS = 16
MOE_TILE = 256
VMEM_LIMIT = 58 * 1024 * 1024


def _rms(x, g):
    return x * lax.rsqrt(jnp.mean(x * x, axis=-1, keepdims=True) + NORM_EPS) * g


def _dot(a, b):
    return jnp.dot(a, b, preferred_element_type=F32)


def _dot_nt(a, b):
    return lax.dot_general(a, b, (((1,), (1,)), ((), ())), preferred_element_type=F32)


def _dot_tn(a, b):
    return lax.dot_general(a, b, (((0,), (0,)), ((), ())), preferred_element_type=F32)


def _params(*sem):
    return pltpu.CompilerParams(dimension_semantics=sem, vmem_limit_bytes=VMEM_LIMIT)


def _split_bf16(x):
    hi = x.astype(BF16)
    return hi, (x - hi.astype(F32)).astype(BF16)


def _to_token_layout(ref, x):
    n = x.shape[0]
    for j in range(TOKEN_TILES):
        ref[pl.ds(j, n, stride=TOKEN_TILES), :] = x[:, j * LANES:(j + 1) * LANES]


def _from_token_layout(ref, n):
    return jnp.concatenate([ref[pl.ds(j, n, stride=TOKEN_TILES), :] for j in range(TOKEN_TILES)], axis=1)


def _rope_slab(x, cos, sin, first_half):
    fwd = pltpu.roll(x, 32, 1)
    bwd = pltpu.roll(x, LANES - 32, 1)
    return x * cos + jnp.where(first_half, bwd, fwd) * sin


def _in_proj_kernel(h_ref, n1_ref, w_ref, cos_ref, sin_ref, z_ref):
    hn = _rms(h_ref[...], n1_ref[...]).astype(BF16)
    cos = cos_ref[...]
    sin = sin_ref[...]
    lane = lax.broadcasted_iota(I32, (hn.shape[0], LANES), 1)
    first_half = (lane & 32) == 0
    for s in range(IN_WIDTH // SECTION):
        zs = _dot(hn, w_ref[:, s * SECTION:(s + 1) * SECTION])
        if s in (1, 2):
            scale = 1.0 if s == 1 else RET_HEAD_DIM ** -0.5
            for c in range(SECTION // LANES):
                slab = _rope_slab(zs[:, c * LANES:(c + 1) * LANES], cos, sin, first_half)
                z_ref[:, s * SECTION + c * LANES:s * SECTION + (c + 1) * LANES] = slab * scale
        else:
            z_ref[:, s * SECTION:(s + 1) * SECTION] = zs


def _in_proj(h, n1, w_in, i, cos_t, sin_t, tm):
    T = h.shape[0]
    n_tab = cos_t.shape[0] // tm if cos_t.shape[0] > 1 else 1
    tb = tm if cos_t.shape[0] > 1 else 1
    return pl.pallas_call(
        _in_proj_kernel,
        grid=(T // tm,),
        in_specs=[
            pl.BlockSpec((tm, D_MODEL), lambda t: (t, 0)),
            pl.BlockSpec((None, 1, D_MODEL), lambda t: (i, 0, 0)),
            pl.BlockSpec((None, D_MODEL, IN_WIDTH), lambda t: (i, 0, 0)),
            pl.BlockSpec((tb, LANES), lambda t: (t % n_tab, 0)),
            pl.BlockSpec((tb, LANES), lambda t: (t % n_tab, 0)),
        ],
        out_specs=pl.BlockSpec((tm, IN_WIDTH), lambda t: (t, 0)),
        out_shape=jax.ShapeDtypeStruct((T, IN_WIDTH), F32),
        compiler_params=_params("arbitrary"),
        name="in_proj",
    )(h, n1, w_in, cos_t, sin_t)


def _top2(lg, idx, axis, n_idx):
    neg = jnp.float32(-jnp.inf)
    m1 = jnp.max(lg, axis=axis, keepdims=True)
    i1 = jnp.min(jnp.where(lg == m1, idx, n_idx), axis=axis, keepdims=True)
    lg2 = jnp.where(idx == i1, neg, lg)
    m2 = jnp.max(lg2, axis=axis, keepdims=True)
    i2 = jnp.min(jnp.where(lg2 == m2, idx, n_idx), axis=axis, keepdims=True)
    e2 = jnp.exp(m2 - m1)
    den = 1.0 + e2
    return i1, i2, 1.0 / den, e2 / den


def _route(x, rhi_ref, rlo_ref):
    x_hi, x_lo = _split_bf16(x)
    logits = _dot(x_hi, rhi_ref[...]) + _dot(x_lo, rhi_ref[...]) + _dot(x_hi, rlo_ref[...])
    lane = lax.broadcasted_iota(I32, logits.shape, 1)
    lg = jnp.where(lane < N_EXPERTS, logits, jnp.float32(-jnp.inf))
    i1, i2, w1, w2 = _top2(lg, lane, 1, LANES)
    return jnp.where(lane == i1, w1, 0.0) + jnp.where(lane == i2, w2, 0.0)


def _route_sorted(x, rthi_ref, rtlo_ref, carry_ref, meta_ref, cnt_ref):
    n = x.shape[0]
    x_hi, x_lo = _split_bf16(x)
    lt = _dot_nt(rthi_ref[...], x_hi) + _dot_nt(rthi_ref[...], x_lo) + _dot_nt(rtlo_ref[...], x_hi)
    sub = lax.broadcasted_iota(I32, lt.shape, 0)
    i1, i2, w1, w2 = _top2(lt, sub, 0, N_EXPERTS)
    ind = jnp.where(sub == i1, 1.0, 0.0) + jnp.where(sub == i2, 1.0, 0.0)
    tri = jnp.where(lax.broadcasted_iota(I32, (n, n), 0) <= lax.broadcasted_iota(I32, (n, n), 1), 1.0, 0.0)
    csum = _dot(ind.astype(BF16), tri.astype(BF16))
    carry = carry_ref[...]
    rank = carry[:, 0:1] + csum - ind
    r1 = jnp.sum(jnp.where(sub == i1, rank, 0.0), axis=0, keepdims=True).astype(I32)
    r2 = jnp.sum(jnp.where(sub == i2, rank, 0.0), axis=0, keepdims=True).astype(I32)
    carry_ref[...] = carry + csum[:, n - 1:n]
    cnt_ref[0] = carry_ref[...].astype(I32)
    rows = [i1, i2, r1, r2, pltpu.bitcast(w1, I32), pltpu.bitcast(w2, I32)]
    meta = jnp.zeros(lt.shape, I32)
    for j, rowv in enumerate(rows):
        meta = jnp.where(sub == j, jnp.broadcast_to(rowv, lt.shape), meta)
    meta_ref[0] = meta


def _pool_project(d_slabs, u, wpool_ref, pscale_ref):
    ys = []
    for g in range(len(POOL_WINDOWS)):
        sl = slice(g * POOL_GROUP_DIM, (g + 1) * POOL_GROUP_DIM)
        ys.append(_dot((d_slabs[g] - u[:, sl]).astype(BF16), wpool_ref[g]) * pscale_ref[:, sl])
    return ys


def _mixer_prompt_kernel(*refs, cb, decay_c, moe):
    (u_ref, q_ref, k_ref, v_ref, g_ref, h_ref, dmask_ref, xi_ref, zeta_ref,
     wpool_ref, pscale_ref, wo_ref, n2_ref) = refs[:13]
    if moe:
        rthi_ref, rtlo_ref, h1_ref, xt_ref, meta_ref, cnt_ref, pool_out_ref, ret_out_ref, s_scr, p_scr, carry_scr = refs[13:]
    else:
        h1_ref, hn2_ref, pool_out_ref, ret_out_ref, s_scr, p_scr = refs[13:]

    c = pl.program_id(1)

    @pl.when(c == 0)
    def _():
        s_scr[...] = jnp.zeros_like(s_scr)
        p_scr[0:TAIL_ROWS, :] = jnp.zeros((TAIL_ROWS, POOL_WIDTH), F32)
        if moe:
            carry_scr[...] = jnp.zeros_like(carry_scr)

    u = u_ref[...]
    p_scr[TAIL_ROWS:TAIL_ROWS + cb, :] = u

    pos = c * cb + lax.broadcasted_iota(I32, (cb, POOL_GROUP_DIM), 0)
    means = []
    for g, w in enumerate(POOL_WINDOWS):
        sl = slice(g * POOL_GROUP_DIM, (g + 1) * POOL_GROUP_DIM)
        s = u[:, sl]
        for j in range(1, w):
            s = s + p_scr[TAIL_ROWS - j:TAIL_ROWS - j + cb, sl]
        means.append(s / jnp.minimum(pos + 1, w).astype(F32))
    ys = _pool_project(means, u, wpool_ref, pscale_ref)
    pool_out_ref[0] = p_scr[cb + TAIL_ROWS - POOL_BUF:cb + TAIL_ROWS, :]
    p_scr[0:TAIL_ROWS, :] = p_scr[cb:cb + TAIL_ROWS, :]

    q = q_ref[...]
    k = k_ref[...]
    qb = q.astype(BF16)
    kb = k.astype(BF16)
    kz = (k * zeta_ref[...]).astype(BF16)
    vb = v_ref[...].astype(BF16)
    xi = xi_ref[...]
    os_ = []
    for hd in range(RET_HEADS):
        sl = slice(hd * RET_HEAD_DIM, (hd + 1) * RET_HEAD_DIM)
        qh, kh, vh = qb[:, sl], kb[:, sl], vb[:, sl]
        sc = _dot_nt(qh, kh) * dmask_ref[hd]
        s_prev = s_scr[hd]
        o_h = _dot(sc.astype(BF16), vh) + _dot(qh, s_prev.astype(BF16)) * xi[:, sl]
        s_scr[hd] = s_prev * decay_c[hd] + _dot_tn(kz[:, sl], vh)
        mu = jnp.mean(o_h, axis=-1, keepdims=True)
        var = jnp.mean(jnp.square(o_h - mu), axis=-1, keepdims=True)
        os_.append((o_h - mu) * lax.rsqrt(var + NORM_EPS))
    ret_out_ref[0] = s_scr[...]
    o = jnp.concatenate(os_, axis=1) * jax.nn.silu(g_ref[...])
    mix = jnp.concatenate(ys + [o], axis=1)

    h1 = h_ref[...] + _dot(mix.astype(BF16), wo_ref[...])
    h1_ref[...] = h1
    hn2 = _rms(h1, n2_ref[...])
    if moe:
        _to_token_layout(xt_ref, hn2)
        _route_sorted(hn2, rthi_ref, rtlo_ref, carry_scr, meta_ref, cnt_ref)
    else:
        hn2_ref[...] = hn2.astype(BF16)


def _mixer_prompt(z, h, tabs, W, i, B, L, cb):
    T = B * L
    nc = L // cb
    moe = i % 2 == 1
    row = lambda b, c: b * nc + c
    zspec = lambda s: pl.BlockSpec((cb, SECTION), lambda b, c, s=s: (row(b, c), s))
    const2 = lambda shape: pl.BlockSpec(shape, lambda b, c: (0, 0))
    const3 = lambda shape: pl.BlockSpec(shape, lambda b, c: (0, 0, 0))
    layer3 = lambda shape: pl.BlockSpec((None,) + shape, lambda b, c: (i, 0, 0))
    rows = pl.BlockSpec((cb, D_MODEL), lambda b, c: (row(b, c), 0))
    in_specs = [zspec(0), zspec(1), zspec(2), zspec(3), zspec(4), rows,
                const3((RET_HEADS, cb, cb)), const2((cb, RET_WIDTH)), const2((cb, RET_WIDTH)),
                pl.BlockSpec((None, len(POOL_WINDOWS), POOL_GROUP_DIM, POOL_GROUP_DIM), lambda b, c: (i, 0, 0, 0)),
                layer3((1, POOL_WIDTH)), layer3((D_MODEL, D_MODEL)), layer3((1, D_MODEL))]
    args = [z, z, z, z, z, h, tabs["dmask"], tabs["xi"], tabs["zeta"],
            W["w_pool"], W["pool_scale"], W["w_o"], W["norm2"]]
    scratch = [pltpu.VMEM((RET_HEADS, RET_HEAD_DIM, RET_HEAD_DIM), F32),
               pltpu.VMEM((TAIL_ROWS + cb, POOL_WIDTH), F32)]
    state_specs = [pl.BlockSpec((1, POOL_BUF, POOL_WIDTH), lambda b, c: (b, 0, 0)),
                   pl.BlockSpec((1, RET_HEADS, RET_HEAD_DIM, RET_HEAD_DIM), lambda b, c: (b, 0, 0, 0))]
    state_shapes = [jax.ShapeDtypeStruct((B, POOL_BUF, POOL_WIDTH), F32),
                    jax.ShapeDtypeStruct((B, RET_HEADS, RET_HEAD_DIM, RET_HEAD_DIM), F32)]
    if moe:
        j = i // 2
        rt = pl.BlockSpec((None, N_EXPERTS, D_MODEL), lambda b, c: (j, 0, 0))
        in_specs += [rt, rt]
        args += [W["router_t_hi"], W["router_t_lo"]]
        out_specs = [rows,
                     pl.BlockSpec((cb * TOKEN_TILES, LANES), lambda b, c: (row(b, c), 0)),
                     pl.BlockSpec((1, SUBLANES, cb), lambda b, c: (row(b, c), 0, 0)),
                     pl.BlockSpec((1, N_EXPERTS, LANES), lambda b, c: (b, 0, 0))]
        out_shape = [jax.ShapeDtypeStruct((T, D_MODEL), F32),
                     jax.ShapeDtypeStruct((T * TOKEN_TILES, LANES), F32),
                     jax.ShapeDtypeStruct((B * nc, SUBLANES, cb), I32),
                     jax.ShapeDtypeStruct((B, N_EXPERTS, LANES), I32)]
        scratch.append(pltpu.VMEM((N_EXPERTS, LANES), F32))
    else:
        out_specs = [rows, rows]
        out_shape = [jax.ShapeDtypeStruct((T, D_MODEL), F32), jax.ShapeDtypeStruct((T, D_MODEL), BF16)]
    return pl.pallas_call(
        functools.partial(_mixer_prompt_kernel, cb=cb, decay_c=tabs["decay_c"], moe=moe),
        grid=(B, nc),
        in_specs=in_specs,
        out_specs=out_specs + state_specs,
        out_shape=out_shape + state_shapes,
        scratch_shapes=scratch,
        compiler_params=_params("arbitrary", "arbitrary"),
        name="mixer_prompt",
    )(*args)


def _column(row, eye):
    return jnp.sum(jnp.where(eye, jnp.broadcast_to(row, eye.shape), 0.0), axis=1, keepdims=True)


def _mixer_sample_kernel(*refs, bb, decay_1, moe):
    (u_ref, q_ref, k_ref, v_ref, g_ref, h_ref, buf_ref, s_ref,
     wpool_ref, pscale_ref, wo_ref, n2_ref) = refs[:12]
    n_in = 14 if moe else 12
    router_refs = refs[12:n_in]
    h1_ref, hn2_ref = refs[n_in:n_in + 2]
    gates_ref = refs[n_in + 2] if moe else None
    buf_out_ref, s_out_ref, o_scr = refs[-3:]

    u = u_ref[...]
    means = []
    for g, w in enumerate(POOL_WINDOWS):
        s = u[:, g * POOL_GROUP_DIM:(g + 1) * POOL_GROUP_DIM]
        for j in range(1, w):
            lo = (POOL_BUF - j) * POOL_WIDTH + g * POOL_GROUP_DIM
            s = s + buf_ref[:, lo:lo + POOL_GROUP_DIM]
        means.append(s / float(min(PAST_LEN + 1, w)))
    ys = _pool_project(means, u, wpool_ref, pscale_ref)
    keep = (POOL_BUF - 1) * POOL_WIDTH
    buf_out_ref[:, 0:keep] = buf_ref[:, POOL_WIDTH:POOL_WIDTH + keep]
    buf_out_ref[:, keep:keep + POOL_WIDTH] = u

    dh = RET_HEAD_DIM
    eye = (lax.broadcasted_iota(I32, (dh, dh), 0) == lax.broadcasted_iota(I32, (dh, dh), 1))

    def per_seq(b, carry):
        qr = q_ref[pl.ds(b, 1), :]
        kr = k_ref[pl.ds(b, 1), :]
        vr = v_ref[pl.ds(b, 1), :]
        heads = []
        for hd in range(RET_HEADS):
            sl = slice(hd * dh, (hd + 1) * dh)
            qh, kh, vh = qr[:, sl], kr[:, sl], vr[:, sl]
            s_prev = s_ref[b, hd]
            cross = jnp.sum(s_prev * _column(qh, eye), axis=0, keepdims=True)
            o_h = jnp.sum(qh * kh, axis=-1, keepdims=True) * vh + cross * decay_1[hd]
            s_out_ref[b, hd] = s_prev * decay_1[hd] + _column(kh, eye) * vh
            mu = jnp.mean(o_h, axis=-1, keepdims=True)
            var = jnp.mean(jnp.square(o_h - mu), axis=-1, keepdims=True)
            heads.append((o_h - mu) * lax.rsqrt(var + NORM_EPS))
        o_scr[pl.ds(b, 1), :] = jnp.concatenate(heads, axis=1)
        return carry

    lax.fori_loop(0, bb, per_seq, 0)
    o = o_scr[...] * jax.nn.silu(g_ref[...])
    mix = jnp.concatenate(ys + [o], axis=1)
    h1 = h_ref[...] + _dot(mix.astype(BF16), wo_ref[...])
    h1_ref[...] = h1
    hn2 = _rms(h1, n2_ref[...])
    hn2_ref[...] = hn2.astype(BF16)
    if moe:
        gates_ref[...] = _route(hn2, *router_refs)


def _mixer_sample(z, h, buf, s0, decay_1, W, i, bb):
    B = h.shape[0]
    moe = i % 2 == 1
    flat = POOL_BUF * POOL_WIDTH
    zspec = lambda s: pl.BlockSpec((bb, SECTION), lambda t, s=s: (t, s))
    layer3 = lambda shape: pl.BlockSpec((None,) + shape, lambda t: (i, 0, 0))
    rows = pl.BlockSpec((bb, D_MODEL), lambda t: (t, 0))
    state = pl.BlockSpec((bb, RET_HEADS, RET_HEAD_DIM, RET_HEAD_DIM), lambda t: (t, 0, 0, 0))
    in_specs = [zspec(0), zspec(1), zspec(2), zspec(3), zspec(4), rows,
                pl.BlockSpec((bb, flat), lambda t: (t, 0)), state,
                pl.BlockSpec((None, len(POOL_WINDOWS), POOL_GROUP_DIM, POOL_GROUP_DIM), lambda t: (i, 0, 0, 0)),
                layer3((1, POOL_WIDTH)), layer3((D_MODEL, D_MODEL)), layer3((1, D_MODEL))]
    args = [z, z, z, z, z, h, buf, s0, W["w_pool"], W["pool_scale"], W["w_o"], W["norm2"]]
    out_specs = [rows, rows]
    out_shape = [jax.ShapeDtypeStruct((B, D_MODEL), F32), jax.ShapeDtypeStruct((B, D_MODEL), BF16)]
    if moe:
        j = i // 2
        r = pl.BlockSpec((None, D_MODEL, LANES), lambda t: (j, 0, 0))
        in_specs += [r, r]
        args += [W["router_hi"], W["router_lo"]]
        out_specs.append(pl.BlockSpec((bb, LANES), lambda t: (t, 0)))
        out_shape.append(jax.ShapeDtypeStruct((B, LANES), F32))
    out_specs += [pl.BlockSpec((bb, flat), lambda t: (t, 0)), state]
    out_shape += [jax.ShapeDtypeStruct((B, flat), F32), jax.ShapeDtypeStruct(s0.shape, F32)]
    outs = pl.pallas_call(
        functools.partial(_mixer_sample_kernel, bb=bb, decay_1=decay_1, moe=moe),
        grid=(B // bb,),
        in_specs=in_specs,
        out_specs=out_specs,
        out_shape=out_shape,
        scratch_shapes=[pltpu.VMEM((bb, RET_WIDTH), F32)],
        compiler_params=_params("arbitrary"),
        name="mixer_sample",
    )(*args)
    if moe:
        h1, hn2, gates, pool, ret = outs
    else:
        (h1, hn2, pool, ret), gates = outs, None
    return h1, hn2, gates, pool, ret


def _swiglu(x, wg_ref, wu_ref, wd_ref):
    hidden = jax.nn.silu(_dot(x, wg_ref[...])) * _dot(x, wu_ref[...])
    return _dot(hidden.astype(BF16), wd_ref[...])


def _ffn_kernel(*refs, moe):
    if moe:
        x_ref, h_ref, gates_ref, wg_ref, wu_ref, wd_ref, out_ref, acc_ref = refs
    else:
        x_ref, h_ref, wg_ref, wu_ref, wd_ref, out_ref, acc_ref = refs
    f = pl.program_id(1)
    y = _swiglu(x_ref[...], wg_ref, wu_ref, wd_ref)
    if moe:
        gates = gates_ref[...]
        lane = lax.broadcasted_iota(I32, gates.shape, 1)
        y = y * jnp.sum(jnp.where(lane == f, gates, 0.0), axis=-1, keepdims=True)

    @pl.when(f == 0)
    def _():
        acc_ref[...] = y

    @pl.when(f > 0)
    def _():
        acc_ref[...] += y

    @pl.when(f == pl.num_programs(1) - 1)
    def _():
        out_ref[...] = h_ref[...] + acc_ref[...]


def _ffn(x, h, gates, wg, wu, wd, j, tm):
    T = x.shape[0]
    moe = gates is not None
    tf = D_FF_TILE
    row = pl.BlockSpec((tm, D_MODEL), lambda t, f: (t, 0))
    in_specs = [row, row]
    args = [x, h]
    if moe:
        nf = wg.shape[1]
        in_specs.append(pl.BlockSpec((tm, LANES), lambda t, f: (t, 0)))
        args.append(gates)
        in_specs += [pl.BlockSpec((None, None, D_MODEL, tf), lambda t, f: (j, f, 0, 0)),
                     pl.BlockSpec((None, None, D_MODEL, tf), lambda t, f: (j, f, 0, 0)),
                     pl.BlockSpec((None, None, tf, D_MODEL), lambda t, f: (j, f, 0, 0))]
    else:
        nf = wg.shape[2] // tf
        in_specs += [pl.BlockSpec((None, D_MODEL, tf), lambda t, f: (j, 0, f)),
                     pl.BlockSpec((None, D_MODEL, tf), lambda t, f: (j, 0, f)),
                     pl.BlockSpec((None, tf, D_MODEL), lambda t, f: (j, f, 0))]
    args += [wg, wu, wd]
    return pl.pallas_call(
        functools.partial(_ffn_kernel, moe=moe),
        grid=(T // tm, nf),
        in_specs=in_specs,
        out_specs=row,
        out_shape=jax.ShapeDtypeStruct((T, D_MODEL), F32),
        scratch_shapes=[pltpu.VMEM((tm, D_MODEL), F32)],
        compiler_params=_params("arbitrary", "arbitrary"),
        name="ffn_moe" if moe else "ffn_dense",
    )(*args)


def _moe_sparse_kernel(dest_ref, w1_ref, w2_ref, tile_e_ref, tile_rows_ref,
                       xt_ref, wg_ref, wu_ref, wd_ref, out_ref,
                       src_ref, ws_ref, g_ref, y_ref, *, tb):
    b = pl.program_id(0)
    k = pl.program_id(1)
    ts = MOE_TILE
    unroll = SUBLANES
    tile8 = lambda r: pl.ds(pl.multiple_of(r * TOKEN_TILES, TOKEN_TILES), TOKEN_TILES)

    @pl.when((b == 0) & (k == 0))
    def _():
        g_ref[...] = jnp.zeros_like(g_ref)

    @pl.when(k == 0)
    def _():
        def fill(t, carry):
            d = dest_ref[b * tb + t]
            d1 = d & 0xFFFF
            d2 = d >> 16
            src_ref[d1] = t
            src_ref[d2] = t | (1 << 16)
            ws_ref[d1] = w1_ref[b * tb + t]
            ws_ref[d2] = w2_ref[b * tb + t]
            return carry

        lax.fori_loop(0, tb, fill, 0, unroll=unroll)

    n_rows = tile_rows_ref[b * pl.num_programs(1) + k]

    @pl.when(n_rows > 0)
    def _():
        base = k * ts
        n_groups = (n_rows + unroll - 1) // unroll

        def gather(g, carry):
            for u in range(unroll):
                r = jnp.minimum(g * unroll + u, n_rows - 1)
                g_ref[tile8(r), :] = xt_ref[tile8(src_ref[base + r] & 0xFFFF), :]
            return carry

        lax.fori_loop(0, n_groups, gather, 0)
        x = _from_token_layout(g_ref, ts).astype(BF16)
        _to_token_layout(y_ref, _swiglu(x, wg_ref, wu_ref, wd_ref))

        def scatter(g, carry):
            for u in range(unroll):
                r = jnp.minimum(g * unroll + u, n_rows - 1)
                s = src_ref[base + r]
                out_ref[s >> 16, tile8(s & 0xFFFF), :] = ws_ref[base + r] * y_ref[tile8(r), :]
            return carry

        lax.fori_loop(0, n_groups, scatter, 0)


def _moe_plan(meta, cnt, B, L, cb):
    ts = MOE_TILE
    n_tiles = TOP_K * L // ts + N_EXPERTS
    m = meta.reshape(B, L // cb, SUBLANES, cb).transpose(2, 0, 1, 3).reshape(SUBLANES, B, L)
    e1, e2, r1, r2 = m[0], m[1], m[2], m[3]
    w1 = lax.bitcast_convert_type(m[4], F32).reshape(B * L)
    w2 = lax.bitcast_convert_type(m[5], F32).reshape(B * L)
    cnt = cnt[:, :, 0]
    padded = (cnt + ts - 1) // ts * ts
    ends = jnp.cumsum(padded, axis=1)
    off = ends - padded
    d1 = jnp.take_along_axis(off, e1, axis=1) + r1
    d2 = jnp.take_along_axis(off, e2, axis=1) + r2
    dest = (d1 | (d2 << 16)).reshape(B * L).astype(I32)
    start = jnp.arange(n_tiles, dtype=I32) * ts
    tile_e = jnp.sum((ends[:, None, :] <= start[None, :, None]).astype(I32), axis=2)
    last_e = jnp.take_along_axis(tile_e, (ends[:, -1:] // ts - 1), axis=1)
    valid = start[None, :] < ends[:, -1:]
    tile_e = jnp.where(valid, jnp.minimum(tile_e, N_EXPERTS - 1), last_e)
    filled = jnp.take_along_axis(cnt, tile_e, axis=1) - (start[None, :] - jnp.take_along_axis(off, tile_e, axis=1))
    tile_rows = jnp.where(valid, jnp.clip(filled, 0, ts), 0)
    return dest, w1, w2, tile_e.reshape(-1).astype(I32), tile_rows.reshape(-1).astype(I32), n_tiles


def _moe_sparse(xt, plan, wg, wu, wd, j, B, L):
    dest, w1, w2, tile_e, tile_rows, n_tiles = plan
    ts = MOE_TILE
    tf = D_FF_TILE
    tok = pl.BlockSpec((L * TOKEN_TILES, LANES), lambda b, k, *_: (b, 0), pipeline_mode=pl.Buffered(1))
    tok_out = pl.BlockSpec((TOP_K, L * TOKEN_TILES, LANES), lambda b, k, *_: (0, b, 0),
                           pipeline_mode=pl.Buffered(1))
    expert = lambda b, k, dest, w1, w2, te, tr: (j, te[b * n_tiles + k], 0, 0)
    grid_spec = pltpu.PrefetchScalarGridSpec(
        num_scalar_prefetch=5,
        grid=(B, n_tiles),
        in_specs=[tok,
                  pl.BlockSpec((None, None, D_MODEL, tf), expert),
                  pl.BlockSpec((None, None, D_MODEL, tf), expert),
                  pl.BlockSpec((None, None, tf, D_MODEL), expert)],
        out_specs=tok_out,
        scratch_shapes=[pltpu.SMEM((n_tiles * ts,), I32), pltpu.SMEM((n_tiles * ts,), F32),
                        pltpu.VMEM((ts * TOKEN_TILES, LANES), F32), pltpu.VMEM((ts * TOKEN_TILES, LANES), F32)],
    )
    return pl.pallas_call(
        functools.partial(_moe_sparse_kernel, tb=L),
        grid_spec=grid_spec,
        out_shape=jax.ShapeDtypeStruct((TOP_K,) + xt.shape, F32),
        compiler_params=_params("arbitrary", "arbitrary"),
        name="moe_sparse",
    )(dest, w1, w2, tile_e, tile_rows, xt, wg, wu, wd)


def _ple_kernel(*refs, final, add_tokens):
    refs = list(refs)
    h_ref = refs.pop(0)
    m_ref = refs.pop(0) if add_tokens else None
    p_ref, wg_ref, wp_ref = refs[:3]
    fn_ref = refs[3] if final else None
    out_ref = refs[-1]
    h = h_ref[...]
    if add_tokens:
        for s in range(TOP_K):
            h = h + _from_token_layout(m_ref.at[s], h.shape[0])
    gate = jax.nn.sigmoid(_dot(h.astype(BF16), wg_ref[...]))
    e = _dot(p_ref[...].astype(BF16), wp_ref[...])
    h3 = h + gate * e
    out_ref[...] = _rms(h3, fn_ref[...]) if final else h3


def _ple(h, m_t, p, W, i, tm):
    T = h.shape[0]
    final = i == DEPTH - 1
    add_tokens = m_t is not None
    rows = pl.BlockSpec((tm, D_MODEL), lambda t: (t, 0))
    in_specs = [rows]
    args = [h]
    if add_tokens:
        in_specs.append(pl.BlockSpec((TOP_K, tm * TOKEN_TILES, LANES), lambda t: (0, t, 0)))
        args.append(m_t)
    in_specs += [pl.BlockSpec((None, tm, PLE_DIM), lambda t: (i, t, 0)),
                 pl.BlockSpec((None, D_MODEL, D_MODEL), lambda t: (i, 0, 0)),
                 pl.BlockSpec((None, PLE_DIM, D_MODEL), lambda t: (i, 0, 0))]
    args += [p, W["w_ple_gate"], W["w_ple"]]
    if final:
        in_specs.append(pl.BlockSpec((1, D_MODEL), lambda t: (0, 0)))
        args.append(W["final_norm"])
    return pl.pallas_call(
        functools.partial(_ple_kernel, final=final, add_tokens=add_tokens),
        grid=(T // tm,),
        in_specs=in_specs,
        out_specs=rows,
        out_shape=jax.ShapeDtypeStruct((T, D_MODEL), F32),
        compiler_params=_params("arbitrary"),
        name="ple_final" if final else "ple",
    )(*args)


def _rope_tables(pos):
    half = RET_HEAD_DIM // 2
    inv = ROPE_THETA ** (-np.arange(half, dtype=np.float64) / half)
    ang = np.asarray(pos, np.float64)[:, None] * inv[None, :]
    cos, sin = np.cos(ang), np.sin(ang)
    reps = LANES // RET_HEAD_DIM
    return (jnp.asarray(np.tile(np.concatenate([cos, cos], axis=-1), (1, reps)), F32),
            jnp.asarray(np.tile(np.concatenate([-sin, sin], axis=-1), (1, reps)), F32))


def _log_decay():
    return np.log1p(-np.power(2.0, -5.0 - np.arange(RET_HEADS, dtype=np.float64)))


def _retention_tables(C):
    log_g = _log_decay()
    idx = np.arange(C, dtype=np.float64)
    rel = idx[:, None] - idx[None, :]
    dmask = np.where(rel >= 0, np.exp(log_g[:, None, None] * np.maximum(rel, 0.0)), 0.0)
    xi = np.exp(log_g[None, :] * (idx[:, None] + 1.0))
    zeta = np.exp(log_g[None, :] * (C - 1.0 - idx)[:, None])
    spread = lambda t: jnp.asarray(np.repeat(t, RET_HEAD_DIM, axis=1), F32)
    return {"dmask": jnp.asarray(dmask, F32), "xi": spread(xi), "zeta": spread(zeta), "decay_c": _chunk_decay(C)}


def _chunk_decay(C):
    return tuple(math.exp(lg * C) for lg in _log_decay())


def _choose_tile(T, cap):
    tm = min(T, cap)
    while T % tm:
        tm //= 2
    return tm


def _layer_prompt(i, h, p, W, rope, tabs, B, L):
    T = B * L
    tm = _choose_tile(T, 512)
    cb = _choose_tile(L, 256)
    j = i // 2
    z = _in_proj(h, W["norm1"], W["w_in"], i, rope[0], rope[1], _choose_tile(L, tm))
    if i % 2 == 0:
        h1, hn2, pool, ret = _mixer_prompt(z, h, tabs, W, i, B, L, cb)
        h2 = _ffn(hn2, h1, None, W["ffn_g"], W["ffn_u"], W["ffn_d"], j, tm)
        m_t = None
    else:
        h2, xt, meta, cnt, pool, ret = _mixer_prompt(z, h, tabs, W, i, B, L, cb)
        m_t = _moe_sparse(xt, _moe_plan(meta, cnt, B, L, cb), W["moe_g"], W["moe_u"], W["moe_d"], j, B, L)
    return _ple(h2, m_t, p, W, i, tm), pool, ret


def _layer_sample(i, h, p, W, rope, decay_1, buf, s0):
    B = h.shape[0]
    j = i // 2
    z = _in_proj(h, W["norm1"], W["w_in"], i, rope[0], rope[1], B)
    h1, hn2, gates, pool, ret = _mixer_sample(z, h, buf, s0, decay_1, W, i, _choose_tile(B, 16))
    if i % 2 == 0:
        h2 = _ffn(hn2, h1, None, W["ffn_g"], W["ffn_u"], W["ffn_d"], j, B)
    else:
        h2 = _ffn(hn2, h1, gates, W["moe_g"], W["moe_u"], W["moe_d"], j, B)
    return _ple(h2, None, p, W, i, B), pool, ret


def _prepare_weights(norm1, w_in, w_pool, pool_scale, w_o, norm2, ffn_w_gate, ffn_w_up, ffn_w_down,
                     moe_router, moe_w_gate, moe_w_up, moe_w_down, w_ple, w_ple_gate, final_norm):
    r_hi, r_lo = _split_bf16(jnp.pad(moe_router, ((0, 0), (0, 0), (0, LANES - N_EXPERTS))))
    rt_hi, rt_lo = _split_bf16(moe_router.transpose(0, 2, 1))
    return {
        "norm1": norm1[:, None, :], "norm2": norm2[:, None, :], "final_norm": final_norm[None, :],
        "pool_scale": pool_scale[:, None, :],
        "w_in": w_in.astype(BF16), "w_pool": w_pool.astype(BF16), "w_o": w_o.astype(BF16),
        "ffn_g": ffn_w_gate.astype(BF16), "ffn_u": ffn_w_up.astype(BF16), "ffn_d": ffn_w_down.astype(BF16),
        "moe_g": moe_w_gate.astype(BF16), "moe_u": moe_w_up.astype(BF16), "moe_d": moe_w_down.astype(BF16),
        "router_hi": r_hi, "router_lo": r_lo, "router_t_hi": rt_hi, "router_t_lo": rt_lo,
        "w_ple": w_ple.astype(BF16), "w_ple_gate": w_ple_gate.astype(BF16),
    }


def _trunk(x_prompt, x_sample, state_pool, state_ret, p_prompt, p_sample, W):
    B, L, _ = x_prompt.shape
    Bs = x_sample.shape[0]
    hp = x_prompt.reshape(B * L, D_MODEL)
    hs = x_sample.reshape(Bs, D_MODEL)
    pp = p_prompt.reshape(DEPTH, B * L, PLE_DIM)
    ps = p_sample.reshape(DEPTH, Bs, PLE_DIM)
    rope_p = _rope_tables(np.arange(L))
    rope_s = _rope_tables(PAST_LEN + np.arange(1))
    tabs = _retention_tables(_choose_tile(L, 256))
    decay_1 = _chunk_decay(1)
    pool_p, ret_p, pool_s, ret_s = [], [], [], []
    for i in range(DEPTH):
        hp, bp, sp = _layer_prompt(i, hp, pp, W, rope_p, tabs, B, L)
        hs, bs, ss = _layer_sample(i, hs, ps, W, rope_s, decay_1,
                                   state_pool[i].reshape(Bs, POOL_BUF * POOL_WIDTH), state_ret[i])
        pool_p.append(bp)
        ret_p.append(sp)
        pool_s.append(bs.reshape(Bs, POOL_BUF, POOL_WIDTH))
        ret_s.append(ss)
    return (hp.reshape(B, L, D_MODEL), hs.reshape(Bs, 1, D_MODEL),
            jnp.stack(pool_p), jnp.stack(ret_p), jnp.stack(pool_s), jnp.stack(ret_s))


def kernel(x_prompt, x_sample, state_pool, state_ret, p_prompt, p_sample, norm1, w_in, w_pool, pool_scale, w_o, norm2, ffn_w_gate, ffn_w_up, ffn_w_down, moe_router, moe_w_gate, moe_w_up, moe_w_down, w_ple, w_ple_gate, final_norm):
    W = _prepare_weights(norm1, w_in, w_pool, pool_scale, w_o, norm2, ffn_w_gate, ffn_w_up, ffn_w_down,
                         moe_router, moe_w_gate, moe_w_up, moe_w_down, w_ple, w_ple_gate, final_norm)
    return _trunk(x_prompt, x_sample, state_pool, state_ret, p_prompt, p_sample, W)
```

```python
import functools
import math

import numpy as np

import jax
import jax.numpy as jnp
from jax import lax
from jax.experimental import pallas as pl
from jax.experimental.pallas import tpu as pltpu

F32 = jnp.float32
BF16 = jnp.bfloat16
I32 = jnp.int32

D_MODEL = 1024
DEPTH = 4
PAST_LEN = 16384
POOL_WIDTH = 512
POOL_WINDOWS = (2, 4, 8, 16)
POOL_GROUP_DIM = 128
POOL_BUF = 15
RET_HEADS = 8
RET_HEAD_DIM = 64
RET_WIDTH = 512
ROPE_THETA = 10000.0
IN_WIDTH = POOL_WIDTH + 4 * RET_WIDTH
SECTION = 512
D_FF_TILE = 1408
N_EXPERTS = 8
TOP_K = 2
PLE_DIM = 256
NORM_EPS = 1e-6
LANES = 128
SUBLANES = 8
TOKEN_TILES = D_MODEL // LANES
assert TOKEN_TILES == SUBLANES
TAIL_ROWS = 16
MOE_TILE = 256
PAR_SEQS = 1
VMEM_LIMIT = 58 * 1024 * 1024


def _rms(x, g):
    return x * lax.rsqrt(jnp.mean(x * x, axis=-1, keepdims=True) + NORM_EPS) * g


def _dot(a, b):
    return jnp.dot(a, b, preferred_element_type=F32)


def _dot_nt(a, b):
    return lax.dot_general(a, b, (((1,), (1,)), ((), ())), preferred_element_type=F32)


def _dot_tn(a, b):
    return lax.dot_general(a, b, (((0,), (0,)), ((), ())), preferred_element_type=F32)


def _params(*sem):
    return pltpu.CompilerParams(dimension_semantics=sem, vmem_limit_bytes=VMEM_LIMIT)


def _split_bf16(x):
    hi = x.astype(BF16)
    return hi, (x - hi.astype(F32)).astype(BF16)


def _to_token_layout(ref, x):
    n = x.shape[0]
    for j in range(TOKEN_TILES):
        ref[pl.ds(j, n, stride=TOKEN_TILES), :] = x[:, j * LANES:(j + 1) * LANES]


def _from_token_layout(ref, n):
    return jnp.concatenate([ref[pl.ds(j, n, stride=TOKEN_TILES), :] for j in range(TOKEN_TILES)], axis=1)


def _rope_slab(x, cos, sin, first_half):
    fwd = pltpu.roll(x, 32, 1)
    bwd = pltpu.roll(x, LANES - 32, 1)
    return x * cos + jnp.where(first_half, bwd, fwd) * sin


def _in_proj_kernel(h_ref, n1_ref, w_ref, cos_ref, sin_ref, z_ref):
    hn = _rms(h_ref[...], n1_ref[...]).astype(BF16)
    cos = cos_ref[...]
    sin = sin_ref[...]
    lane = lax.broadcasted_iota(I32, (hn.shape[0], LANES), 1)
    first_half = (lane & 32) == 0
    for s in range(IN_WIDTH // SECTION):
        zs = _dot(hn, w_ref[:, s * SECTION:(s + 1) * SECTION])
        if s in (1, 2):
            scale = 1.0 if s == 1 else RET_HEAD_DIM ** -0.5
            for c in range(SECTION // LANES):
                slab = _rope_slab(zs[:, c * LANES:(c + 1) * LANES], cos, sin, first_half)
                z_ref[:, s * SECTION + c * LANES:s * SECTION + (c + 1) * LANES] = slab * scale
        else:
            z_ref[:, s * SECTION:(s + 1) * SECTION] = zs


def _in_proj(h, n1, w_in, i, cos_t, sin_t, tm):
    T = h.shape[0]
    n_tab = cos_t.shape[0] // tm if cos_t.shape[0] > 1 else 1
    tb = tm if cos_t.shape[0] > 1 else 1
    return pl.pallas_call(
        _in_proj_kernel,
        grid=(T // tm,),
        in_specs=[
            pl.BlockSpec((tm, D_MODEL), lambda t: (t, 0)),
            pl.BlockSpec((None, 1, D_MODEL), lambda t: (i, 0, 0)),
            pl.BlockSpec((None, D_MODEL, IN_WIDTH), lambda t: (i, 0, 0)),
            pl.BlockSpec((tb, LANES), lambda t: (t % n_tab, 0)),
            pl.BlockSpec((tb, LANES), lambda t: (t % n_tab, 0)),
        ],
        out_specs=pl.BlockSpec((tm, IN_WIDTH), lambda t: (t, 0)),
        out_shape=jax.ShapeDtypeStruct((T, IN_WIDTH), F32),
        compiler_params=_params("arbitrary"),
        name="in_proj",
    )(h, n1, w_in, cos_t, sin_t)


def _top2(lg, idx, axis, n_idx):
    neg = jnp.float32(-jnp.inf)
    m1 = jnp.max(lg, axis=axis, keepdims=True)
    i1 = jnp.min(jnp.where(lg == m1, idx, n_idx), axis=axis, keepdims=True)
    lg2 = jnp.where(idx == i1, neg, lg)
    m2 = jnp.max(lg2, axis=axis, keepdims=True)
    i2 = jnp.min(jnp.where(lg2 == m2, idx, n_idx), axis=axis, keepdims=True)
    e2 = jnp.exp(m2 - m1)
    den = 1.0 + e2
    return i1, i2, 1.0 / den, e2 / den


def _route(x, rhi_ref, rlo_ref):
    x_hi, x_lo = _split_bf16(x)
    logits = _dot(x_hi, rhi_ref[...]) + _dot(x_lo, rhi_ref[...]) + _dot(x_hi, rlo_ref[...])
    lane = lax.broadcasted_iota(I32, logits.shape, 1)
    lg = jnp.where(lane < N_EXPERTS, logits, jnp.float32(-jnp.inf))
    i1, i2, w1, w2 = _top2(lg, lane, 1, LANES)
    return jnp.where(lane == i1, w1, 0.0) + jnp.where(lane == i2, w2, 0.0)


def _route_sorted(x, rthi_ref, rtlo_ref, carry_ref, meta_ref, cnt_ref):
    n = x.shape[0]
    x_hi, x_lo = _split_bf16(x)
    lt = _dot_nt(rthi_ref[...], x_hi) + _dot_nt(rthi_ref[...], x_lo) + _dot_nt(rtlo_ref[...], x_hi)
    sub = lax.broadcasted_iota(I32, lt.shape, 0)
    i1, i2, w1, w2 = _top2(lt, sub, 0, N_EXPERTS)
    ind = jnp.where(sub == i1, 1.0, 0.0) + jnp.where(sub == i2, 1.0, 0.0)
    tri = jnp.where(lax.broadcasted_iota(I32, (n, n), 0) <= lax.broadcasted_iota(I32, (n, n), 1), 1.0, 0.0)
    csum = _dot(ind.astype(BF16), tri.astype(BF16))
    carry = carry_ref[...]
    rank = carry[:, 0:1] + csum - ind
    r1 = jnp.sum(jnp.where(sub == i1, rank, 0.0), axis=0, keepdims=True).astype(I32)
    r2 = jnp.sum(jnp.where(sub == i2, rank, 0.0), axis=0, keepdims=True).astype(I32)
    carry_ref[...] = carry + csum[:, n - 1:n]
    cnt_ref[0] = carry_ref[...].astype(I32)
    rows = [i1, i2, r1, r2, pltpu.bitcast(w1, I32), pltpu.bitcast(w2, I32)]
    meta = jnp.zeros(lt.shape, I32)
    for j, rowv in enumerate(rows):
        meta = jnp.where(sub == j, jnp.broadcast_to(rowv, lt.shape), meta)
    meta_ref[0] = meta


def _pool_project(d_slabs, u, wpool_ref, pscale_ref):
    ys = []
    for g in range(len(POOL_WINDOWS)):
        sl = slice(g * POOL_GROUP_DIM, (g + 1) * POOL_GROUP_DIM)
        ys.append(_dot((d_slabs[g] - u[:, sl]).astype(BF16), wpool_ref[g]) * pscale_ref[:, sl])
    return ys


def _mixer_prompt_seq(c, seq, shared, cb, decay_c, moe):
    u_ref, q_ref, k_ref, v_ref, g_ref, h_ref = seq[:6]
    dmask_ref, xi_ref, zeta_ref, wpool_ref, pscale_ref, wo_ref, n2_ref = shared[:7]
    if moe:
        rthi_ref, rtlo_ref = shared[7:]
        h1_ref, xt_ref, meta_ref, cnt_ref, pool_out_ref, ret_out_ref, s_scr, p_scr, carry_scr = seq[6:]
    else:
        h1_ref, hn2_ref, pool_out_ref, ret_out_ref, s_scr, p_scr = seq[6:]

    u = u_ref[...]
    p_scr[TAIL_ROWS:TAIL_ROWS + cb, :] = u

    pos = c * cb + lax.broadcasted_iota(I32, (cb, POOL_GROUP_DIM), 0)
    means = []
    for g, w in enumerate(POOL_WINDOWS):
        sl = slice(g * POOL_GROUP_DIM, (g + 1) * POOL_GROUP_DIM)
        s = u[:, sl]
        for j in range(1, w):
            s = s + p_scr[TAIL_ROWS - j:TAIL_ROWS - j + cb, sl]
        means.append(s / jnp.minimum(pos + 1, w).astype(F32))
    ys = _pool_project(means, u, wpool_ref, pscale_ref)
    pool_out_ref[0] = p_scr[cb + TAIL_ROWS - POOL_BUF:cb + TAIL_ROWS, :]
    p_scr[0:TAIL_ROWS, :] = p_scr[cb:cb + TAIL_ROWS, :]

    q = q_ref[...]
    k = k_ref[...]
    qb = q.astype(BF16)
    kb = k.astype(BF16)
    kz = (k * zeta_ref[...]).astype(BF16)
    vb = v_ref[...].astype(BF16)
    xi = xi_ref[...]
    heads = range(RET_HEADS)
    sls = [slice(hd * RET_HEAD_DIM, (hd + 1) * RET_HEAD_DIM) for hd in heads]
    s_prev = [s_scr[hd] for hd in heads]
    sc = [_dot_nt(qb[:, sl], kb[:, sl]) for sl in sls]
    cross = [_dot(qb[:, sl], s_prev[hd].astype(BF16)) for hd, sl in enumerate(sls)]
    upd = [_dot_tn(kz[:, sl], vb[:, sl]) for sl in sls]
    pm = [(sc[hd] * dmask_ref[hd]).astype(BF16) for hd in heads]
    o_h = [_dot(pm[hd], vb[:, sl]) + cross[hd] * xi[:, sl] for hd, sl in enumerate(sls)]
    for hd in heads:
        s_scr[hd] = s_prev[hd] * decay_c[hd] + upd[hd]
    mu = [jnp.mean(o, axis=-1, keepdims=True) for o in o_h]
    cen = [o - m for o, m in zip(o_h, mu)]
    var = [jnp.mean(jnp.square(x), axis=-1, keepdims=True) for x in cen]
    os_ = [x * lax.rsqrt(v + NORM_EPS) for x, v in zip(cen, var)]
    ret_out_ref[0] = s_scr[...]
    o = jnp.concatenate(os_, axis=1) * jax.nn.silu(g_ref[...])
    mix = jnp.concatenate(ys + [o], axis=1)

    h1 = h_ref[...] + _dot(mix.astype(BF16), wo_ref[...])
    h1_ref[...] = h1
    hn2 = _rms(h1, n2_ref[...])
    if moe:
        _to_token_layout(xt_ref, hn2)
        _route_sorted(hn2, rthi_ref, rtlo_ref, carry_scr, meta_ref, cnt_ref)
    else:
        hn2_ref[...] = hn2.astype(BF16)


def _mixer_prompt_kernel(*refs, cb, decay_c, moe, n_par):
    n_shared = 9 if moe else 7
    seq_in, shared, seq_rest = refs[:6], refs[6:6 + n_shared], refs[6 + n_shared:]
    c = pl.program_id(1)
    scratch = seq_rest[-3:] if moe else seq_rest[-2:]

    @pl.when(c == 0)
    def _():
        s_scr, p_scr = scratch[:2]
        s_scr[...] = jnp.zeros_like(s_scr)
        p_scr[:, 0:TAIL_ROWS, :] = jnp.zeros((n_par, TAIL_ROWS, POOL_WIDTH), F32)
        if moe:
            scratch[2][...] = jnp.zeros_like(scratch[2])

    for p in range(n_par):
        seq = [r.at[p] for r in seq_in] + [r.at[p] for r in seq_rest]
        _mixer_prompt_seq(c, seq, shared, cb, decay_c, moe)


def _mixer_prompt(z, h, tabs, W, i, B, L, cb):
    T = B * L
    nc = L // cb
    moe = i % 2 == 1
    npar = PAR_SEQS if B % PAR_SEQS == 0 else 1
    bg = B // npar
    row = lambda b, c: b * nc + c
    zspec = lambda s: pl.BlockSpec((npar, cb, SECTION), lambda b, c, s=s: (0, row(b, c), s))
    const2 = lambda shape: pl.BlockSpec(shape, lambda b, c: (0, 0))
    const3 = lambda shape: pl.BlockSpec(shape, lambda b, c: (0, 0, 0))
    layer3 = lambda shape: pl.BlockSpec((None,) + shape, lambda b, c: (i, 0, 0))
    rows = pl.BlockSpec((npar, cb, D_MODEL), lambda b, c: (0, row(b, c), 0))
    zp = z.reshape(npar, T // npar, IN_WIDTH)
    in_specs = [zspec(0), zspec(1), zspec(2), zspec(3), zspec(4), rows,
                const3((RET_HEADS, cb, cb)), const2((cb, RET_WIDTH)), const2((cb, RET_WIDTH)),
                pl.BlockSpec((None, len(POOL_WINDOWS), POOL_GROUP_DIM, POOL_GROUP_DIM), lambda b, c: (i, 0, 0, 0)),
                layer3((1, POOL_WIDTH)), layer3((D_MODEL, D_MODEL)), layer3((1, D_MODEL))]
    args = [zp, zp, zp, zp, zp, h.reshape(npar, T // npar, D_MODEL), tabs["dmask"], tabs["xi"], tabs["zeta"],
            W["w_pool"], W["pool_scale"], W["w_o"], W["norm2"]]
    scratch = [pltpu.VMEM((npar, RET_HEADS, RET_HEAD_DIM, RET_HEAD_DIM), F32),
               pltpu.VMEM((npar, TAIL_ROWS + cb, POOL_WIDTH), F32)]
    state_specs = [pl.BlockSpec((npar, 1, POOL_BUF, POOL_WIDTH), lambda b, c: (0, b, 0, 0)),
                   pl.BlockSpec((npar, 1, RET_HEADS, RET_HEAD_DIM, RET_HEAD_DIM), lambda b, c: (0, b, 0, 0, 0))]
    state_shapes = [jax.ShapeDtypeStruct((npar, bg, POOL_BUF, POOL_WIDTH), F32),
                    jax.ShapeDtypeStruct((npar, bg, RET_HEADS, RET_HEAD_DIM, RET_HEAD_DIM), F32)]
    if moe:
        j = i // 2
        rt = pl.BlockSpec((None, N_EXPERTS, D_MODEL), lambda b, c: (j, 0, 0))
        in_specs += [rt, rt]
        args += [W["router_t_hi"], W["router_t_lo"]]
        out_specs = [rows,
                     pl.BlockSpec((npar, cb * TOKEN_TILES, LANES), lambda b, c: (0, row(b, c), 0)),
                     pl.BlockSpec((npar, 1, SUBLANES, cb), lambda b, c: (0, row(b, c), 0, 0)),
                     pl.BlockSpec((npar, 1, N_EXPERTS, LANES), lambda b, c: (0, b, 0, 0))]
        out_shape = [jax.ShapeDtypeStruct((npar, T // npar, D_MODEL), F32),
                     jax.ShapeDtypeStruct((npar, T // npar * TOKEN_TILES, LANES), F32),
                     jax.ShapeDtypeStruct((npar, bg * nc, SUBLANES, cb), I32),
                     jax.ShapeDtypeStruct((npar, bg, N_EXPERTS, LANES), I32)]
        scratch.append(pltpu.VMEM((npar, N_EXPERTS, LANES), F32))
    else:
        out_specs = [rows, rows]
        out_shape = [jax.ShapeDtypeStruct((npar, T // npar, D_MODEL), F32),
                     jax.ShapeDtypeStruct((npar, T // npar, D_MODEL), BF16)]
    outs = pl.pallas_call(
        functools.partial(_mixer_prompt_kernel, cb=cb, decay_c=tabs["decay_c"], moe=moe, n_par=npar),
        grid=(bg, nc),
        in_specs=in_specs,
        out_specs=out_specs + state_specs,
        out_shape=out_shape + state_shapes,
        scratch_shapes=scratch,
        compiler_params=_params("arbitrary", "arbitrary"),
        name="mixer_prompt",
    )(*args)
    return [o.reshape((o.shape[0] * o.shape[1],) + o.shape[2:]) for o in outs]


def _to_head_rows(ref, x):
    n = x.shape[0]
    for hd in range(RET_HEADS):
        ref[pl.ds(hd, n, stride=RET_HEADS), 0:RET_HEAD_DIM] = x[:, hd * RET_HEAD_DIM:(hd + 1) * RET_HEAD_DIM]


def _mixer_sample_kernel(*refs, bb, decay_1, moe):
    (u_ref, q_ref, k_ref, v_ref, g_ref, h_ref, buf_ref, s_ref,
     wpool_ref, pscale_ref, wo_ref, n2_ref) = refs[:12]
    n_in = 14 if moe else 12
    router_refs = refs[12:n_in]
    h1_ref, hn2_ref = refs[n_in:n_in + 2]
    gates_ref = refs[n_in + 2] if moe else None
    buf_out_ref, s_out_ref, q2_scr, k2_scr, v2_scr, g2_scr, c2_scr, o2_scr, qrep_scr, krep_scr = refs[-10:]
    dh = RET_HEAD_DIM
    grp = SUBLANES

    u = u_ref[...]
    means = []
    for g, w in enumerate(POOL_WINDOWS):
        s = u[:, g * POOL_GROUP_DIM:(g + 1) * POOL_GROUP_DIM]
        for j in range(1, w):
            lo = (POOL_BUF - j) * POOL_WIDTH + g * POOL_GROUP_DIM
            s = s + buf_ref[:, lo:lo + POOL_GROUP_DIM]
        means.append(s / float(min(PAST_LEN + 1, w)))
    ys = _pool_project(means, u, wpool_ref, pscale_ref)
    keep = (POOL_BUF - 1) * POOL_WIDTH
    buf_out_ref[:, 0:keep] = buf_ref[:, POOL_WIDTH:POOL_WIDTH + keep]
    buf_out_ref[:, keep:keep + POOL_WIDTH] = u

    for ref, src in ((q2_scr, q_ref), (k2_scr, k_ref), (v2_scr, v_ref), (g2_scr, g_ref)):
        _to_head_rows(ref, src[...])
    spread = jnp.where(lax.broadcasted_iota(I32, (grp, grp * LANES), 0)
                       == lax.broadcasted_iota(I32, (grp, grp * LANES), 1) // LANES, 1.0, 0.0)

    def per_group(gi, carry):
        rows = pl.ds(pl.multiple_of(gi * grp, grp), grp)
        qrep_scr[...] = _dot_tn(q_ref[rows, :], spread)
        krep_scr[...] = _dot_tn(k_ref[rows, :], spread)
        for bl in range(grp):
            b = gi * grp + bl
            cross = []
            for hd in range(RET_HEADS):
                tile = (slice(hd * dh, (hd + 1) * dh), slice(bl * LANES, bl * LANES + dh))
                s_prev = s_ref[b, hd]
                cross.append(jnp.sum(s_prev * qrep_scr[tile], axis=0, keepdims=True))
                v_row = v2_scr[pl.ds(b * RET_HEADS + hd, 1), 0:dh]
                s_out_ref[b, hd] = s_prev * decay_1[hd] + krep_scr[tile] * v_row
            c2_scr[pl.ds(pl.multiple_of(b * RET_HEADS, RET_HEADS), RET_HEADS), 0:dh] = jnp.concatenate(cross, axis=0)
        return carry

    lax.fori_loop(0, bb // grp, per_group, 0)

    n2 = bb * RET_HEADS
    q2, k2, v2 = q2_scr[:, 0:dh], k2_scr[:, 0:dh], v2_scr[:, 0:dh]
    head = lax.broadcasted_iota(I32, (n2, dh), 0) % RET_HEADS
    gamma = jnp.zeros((n2, dh), F32)
    for hd in range(RET_HEADS):
        gamma = jnp.where(head == hd, decay_1[hd], gamma)
    o2 = jnp.sum(q2 * k2, axis=-1, keepdims=True) * v2 + gamma * c2_scr[:, 0:dh]
    mu = jnp.mean(o2, axis=-1, keepdims=True)
    var = jnp.mean(jnp.square(o2 - mu), axis=-1, keepdims=True)
    o2_scr[:, 0:dh] = (o2 - mu) * lax.rsqrt(var + NORM_EPS) * jax.nn.silu(g2_scr[:, 0:dh])
    o = [o2_scr[pl.ds(hd, bb, stride=RET_HEADS), 0:dh] for hd in range(RET_HEADS)]
    mix = jnp.concatenate(ys + o, axis=1)
    h1 = h_ref[...] + _dot(mix.astype(BF16), wo_ref[...])
    h1_ref[...] = h1
    hn2 = _rms(h1, n2_ref[...])
    hn2_ref[...] = hn2.astype(BF16)
    if moe:
        gates_ref[...] = _route(hn2, *router_refs)


def _mixer_sample(z, h, buf, s0, decay_1, W, i, bb):
    B = h.shape[0]
    moe = i % 2 == 1
    flat = POOL_BUF * POOL_WIDTH
    zspec = lambda s: pl.BlockSpec((bb, SECTION), lambda t, s=s: (t, s))
    layer3 = lambda shape: pl.BlockSpec((None,) + shape, lambda t: (i, 0, 0))
    rows = pl.BlockSpec((bb, D_MODEL), lambda t: (t, 0))
    state = pl.BlockSpec((bb, RET_HEADS, RET_HEAD_DIM, RET_HEAD_DIM), lambda t: (t, 0, 0, 0))
    in_specs = [zspec(0), zspec(1), zspec(2), zspec(3), zspec(4), rows,
                pl.BlockSpec((bb, flat), lambda t: (t, 0)), state,
                pl.BlockSpec((None, len(POOL_WINDOWS), POOL_GROUP_DIM, POOL_GROUP_DIM), lambda t: (i, 0, 0, 0)),
                layer3((1, POOL_WIDTH)), layer3((D_MODEL, D_MODEL)), layer3((1, D_MODEL))]
    args = [z, z, z, z, z, h, buf, s0, W["w_pool"], W["pool_scale"], W["w_o"], W["norm2"]]
    out_specs = [rows, rows]
    out_shape = [jax.ShapeDtypeStruct((B, D_MODEL), F32), jax.ShapeDtypeStruct((B, D_MODEL), BF16)]
    if moe:
        j = i // 2
        r = pl.BlockSpec((None, D_MODEL, LANES), lambda t: (j, 0, 0))
        in_specs += [r, r]
        args += [W["router_hi"], W["router_lo"]]
        out_specs.append(pl.BlockSpec((bb, LANES), lambda t: (t, 0)))
        out_shape.append(jax.ShapeDtypeStruct((B, LANES), F32))
    out_specs += [pl.BlockSpec((bb, flat), lambda t: (t, 0)), state]
    out_shape += [jax.ShapeDtypeStruct((B, flat), F32), jax.ShapeDtypeStruct(s0.shape, F32)]
    outs = pl.pallas_call(
        functools.partial(_mixer_sample_kernel, bb=bb, decay_1=decay_1, moe=moe),
        grid=(B // bb,),
        in_specs=in_specs,
        out_specs=out_specs,
        out_shape=out_shape,
        scratch_shapes=[pltpu.VMEM((bb * RET_HEADS, LANES), F32) for _ in range(6)]
        + [pltpu.VMEM((RET_WIDTH, SUBLANES * LANES), F32) for _ in range(2)],
        compiler_params=_params("arbitrary"),
        name="mixer_sample",
    )(*args)
    if moe:
        h1, hn2, gates, pool, ret = outs
    else:
        (h1, hn2, pool, ret), gates = outs, None
    return h1, hn2, gates, pool, ret


def _swiglu(x, wg_ref, wu_ref, wd_ref):
    hidden = jax.nn.silu(_dot(x, wg_ref[...])) * _dot(x, wu_ref[...])
    return _dot(hidden.astype(BF16), wd_ref[...])


def _ffn_kernel(*refs, moe):
    if moe:
        x_ref, h_ref, gates_ref, wg_ref, wu_ref, wd_ref, out_ref, acc_ref = refs
    else:
        x_ref, h_ref, wg_ref, wu_ref, wd_ref, out_ref, acc_ref = refs
    f = pl.program_id(1)
    y = _swiglu(x_ref[...], wg_ref, wu_ref, wd_ref)
    if moe:
        gates = gates_ref[...]
        lane = lax.broadcasted_iota(I32, gates.shape, 1)
        y = y * jnp.sum(jnp.where(lane == f, gates, 0.0), axis=-1, keepdims=True)

    @pl.when(f == 0)
    def _():
        acc_ref[...] = y

    @pl.when(f > 0)
    def _():
        acc_ref[...] += y

    @pl.when(f == pl.num_programs(1) - 1)
    def _():
        out_ref[...] = h_ref[...] + acc_ref[...]


def _ffn(x, h, gates, wg, wu, wd, j, tm):
    T = x.shape[0]
    moe = gates is not None
    tf = D_FF_TILE
    row = pl.BlockSpec((tm, D_MODEL), lambda t, f: (t, 0))
    in_specs = [row, row]
    args = [x, h]
    if moe:
        nf = wg.shape[1]
        in_specs.append(pl.BlockSpec((tm, LANES), lambda t, f: (t, 0)))
        args.append(gates)
        in_specs += [pl.BlockSpec((None, None, D_MODEL, tf), lambda t, f: (j, f, 0, 0)),
                     pl.BlockSpec((None, None, D_MODEL, tf), lambda t, f: (j, f, 0, 0)),
                     pl.BlockSpec((None, None, tf, D_MODEL), lambda t, f: (j, f, 0, 0))]
    else:
        nf = wg.shape[2] // tf
        in_specs += [pl.BlockSpec((None, D_MODEL, tf), lambda t, f: (j, 0, f)),
                     pl.BlockSpec((None, D_MODEL, tf), lambda t, f: (j, 0, f)),
                     pl.BlockSpec((None, tf, D_MODEL), lambda t, f: (j, f, 0))]
    args += [wg, wu, wd]
    return pl.pallas_call(
        functools.partial(_ffn_kernel, moe=moe),
        grid=(T // tm, nf),
        in_specs=in_specs,
        out_specs=row,
        out_shape=jax.ShapeDtypeStruct((T, D_MODEL), F32),
        scratch_shapes=[pltpu.VMEM((tm, D_MODEL), F32)],
        compiler_params=_params("arbitrary", "arbitrary"),
        name="ffn_moe" if moe else "ffn_dense",
    )(*args)


def _moe_sparse_kernel(dest_ref, w1_ref, w2_ref, tile_e_ref, tile_rows_ref,
                       xt_ref, wg_ref, wu_ref, wd_ref, out_ref,
                       src_ref, ws_ref, g_ref, y_ref, *, tb):
    b = pl.program_id(0)
    k = pl.program_id(1)
    ts = MOE_TILE
    unroll = SUBLANES
    tile8 = lambda r: pl.ds(pl.multiple_of(r * TOKEN_TILES, TOKEN_TILES), TOKEN_TILES)

    @pl.when((b == 0) & (k == 0))
    def _():
        g_ref[...] = jnp.zeros_like(g_ref)

    @pl.when(k == 0)
    def _():
        def fill(t, carry):
            d = dest_ref[b * tb + t]
            d1 = d & 0xFFFF
            d2 = d >> 16
            src_ref[d1] = t
            src_ref[d2] = t | (1 << 16)
            ws_ref[d1] = w1_ref[b * tb + t]
            ws_ref[d2] = w2_ref[b * tb + t]
            return carry

        lax.fori_loop(0, tb, fill, 0, unroll=unroll)

    n_rows = tile_rows_ref[b * pl.num_programs(1) + k]

    @pl.when(n_rows > 0)
    def _():
        base = k * ts
        n_groups = (n_rows + unroll - 1) // unroll

        def gather(g, carry):
            for u in range(unroll):
                r = jnp.minimum(g * unroll + u, n_rows - 1)
                g_ref[tile8(r), :] = xt_ref[tile8(src_ref[base + r] & 0xFFFF), :]
            return carry

        lax.fori_loop(0, n_groups, gather, 0)
        x = _from_token_layout(g_ref, ts).astype(BF16)
        _to_token_layout(y_ref, _swiglu(x, wg_ref, wu_ref, wd_ref))

        def scatter(g, carry):
            for u in range(unroll):
                r = jnp.minimum(g * unroll + u, n_rows - 1)
                s = src_ref[base + r]
                out_ref[s >> 16, tile8(s & 0xFFFF), :] = ws_ref[base + r] * y_ref[tile8(r), :]
            return carry

        lax.fori_loop(0, n_groups, scatter, 0)


def _moe_plan(meta, cnt, B, L, cb):
    ts = MOE_TILE
    n_tiles = TOP_K * L // ts + N_EXPERTS
    m = meta.reshape(B, L // cb, SUBLANES, cb).transpose(2, 0, 1, 3).reshape(SUBLANES, B, L)
    e1, e2, r1, r2 = m[0], m[1], m[2], m[3]
    w1 = lax.bitcast_convert_type(m[4], F32).reshape(B * L)
    w2 = lax.bitcast_convert_type(m[5], F32).reshape(B * L)
    cnt = cnt[:, :, 0]
    padded = (cnt + ts - 1) // ts * ts
    ends = jnp.cumsum(padded, axis=1)
    off = ends - padded
    experts = jnp.arange(N_EXPERTS, dtype=I32)

    def pick(table, idx):
        return jnp.sum(jnp.where(idx[:, :, None] == experts, table[:, None, :], 0), axis=2)

    d1 = pick(off, e1) + r1
    d2 = pick(off, e2) + r2
    dest = (d1 | (d2 << 16)).reshape(B * L).astype(I32)
    start = jnp.arange(n_tiles, dtype=I32) * ts
    tile_e = jnp.sum((ends[:, None, :] <= start[None, :, None]).astype(I32), axis=2)
    valid = start[None, :] < ends[:, -1:]
    last_e = jnp.max(jnp.where(valid, tile_e, 0), axis=1, keepdims=True)
    tile_e = jnp.where(valid, tile_e, last_e)
    filled = pick(cnt, tile_e) - (start[None, :] - pick(off, tile_e))
    tile_rows = jnp.where(valid, jnp.clip(filled, 0, ts), 0)
    return dest, w1, w2, tile_e.reshape(-1).astype(I32), tile_rows.reshape(-1).astype(I32), n_tiles


def _moe_sparse(xt, plan, wg, wu, wd, j, B, L):
    dest, w1, w2, tile_e, tile_rows, n_tiles = plan
    ts = MOE_TILE
    tf = D_FF_TILE
    tok = pl.BlockSpec((L * TOKEN_TILES, LANES), lambda b, k, *_: (b, 0), pipeline_mode=pl.Buffered(1))
    tok_out = pl.BlockSpec((TOP_K, L * TOKEN_TILES, LANES), lambda b, k, *_: (0, b, 0),
                           pipeline_mode=pl.Buffered(1))
    expert = lambda b, k, dest, w1, w2, te, tr: (j, te[b * n_tiles + k], 0, 0)
    grid_spec = pltpu.PrefetchScalarGridSpec(
        num_scalar_prefetch=5,
        grid=(B, n_tiles),
        in_specs=[tok,
                  pl.BlockSpec((None, None, D_MODEL, tf), expert),
                  pl.BlockSpec((None, None, D_MODEL, tf), expert),
                  pl.BlockSpec((None, None, tf, D_MODEL), expert)],
        out_specs=tok_out,
        scratch_shapes=[pltpu.SMEM((n_tiles * ts,), I32), pltpu.SMEM((n_tiles * ts,), F32),
                        pltpu.VMEM((ts * TOKEN_TILES, LANES), F32), pltpu.VMEM((ts * TOKEN_TILES, LANES), F32)],
    )
    return pl.pallas_call(
        functools.partial(_moe_sparse_kernel, tb=L),
        grid_spec=grid_spec,
        out_shape=jax.ShapeDtypeStruct((TOP_K,) + xt.shape, F32),
        compiler_params=_params("arbitrary", "arbitrary"),
        name="moe_sparse",
    )(dest, w1, w2, tile_e, tile_rows, xt, wg, wu, wd)


def _ple_kernel(*refs, final, add_tokens):
    refs = list(refs)
    h_ref = refs.pop(0)
    m_ref = refs.pop(0) if add_tokens else None
    p_ref, wg_ref, wp_ref = refs[:3]
    fn_ref = refs[3] if final else None
    out_ref = refs[-1]
    h = h_ref[...]
    if add_tokens:
        for s in range(TOP_K):
            h = h + _from_token_layout(m_ref.at[s], h.shape[0])
    gate = jax.nn.sigmoid(_dot(h.astype(BF16), wg_ref[...]))
    e = _dot(p_ref[...].astype(BF16), wp_ref[...])
    h3 = h + gate * e
    out_ref[...] = _rms(h3, fn_ref[...]) if final else h3


def _ple(h, m_t, p, W, i, tm):
    T = h.shape[0]
    final = i == DEPTH - 1
    add_tokens = m_t is not None
    rows = pl.BlockSpec((tm, D_MODEL), lambda t: (t, 0))
    in_specs = [rows]
    args = [h]
    if add_tokens:
        in_specs.append(pl.BlockSpec((TOP_K, tm * TOKEN_TILES, LANES), lambda t: (0, t, 0)))
        args.append(m_t)
    in_specs += [pl.BlockSpec((None, tm, PLE_DIM), lambda t: (i, t, 0)),
                 pl.BlockSpec((None, D_MODEL, D_MODEL), lambda t: (i, 0, 0)),
                 pl.BlockSpec((None, PLE_DIM, D_MODEL), lambda t: (i, 0, 0))]
    args += [p, W["w_ple_gate"], W["w_ple"]]
    if final:
        in_specs.append(pl.BlockSpec((1, D_MODEL), lambda t: (0, 0)))
        args.append(W["final_norm"])
    return pl.pallas_call(
        functools.partial(_ple_kernel, final=final, add_tokens=add_tokens),
        grid=(T // tm,),
        in_specs=in_specs,
        out_specs=rows,
        out_shape=jax.ShapeDtypeStruct((T, D_MODEL), F32),
        compiler_params=_params("arbitrary"),
        name="ple_final" if final else "ple",
    )(*args)


def _rope_tables(pos):
    half = RET_HEAD_DIM // 2
    inv = ROPE_THETA ** (-np.arange(half, dtype=np.float64) / half)
    ang = np.asarray(pos, np.float64)[:, None] * inv[None, :]
    cos, sin = np.cos(ang), np.sin(ang)
    reps = LANES // RET_HEAD_DIM
    return (jnp.asarray(np.tile(np.concatenate([cos, cos], axis=-1), (1, reps)), F32),
            jnp.asarray(np.tile(np.concatenate([-sin, sin], axis=-1), (1, reps)), F32))


def _log_decay():
    return np.log1p(-np.power(2.0, -5.0 - np.arange(RET_HEADS, dtype=np.float64)))


def _retention_tables(C):
    log_g = _log_decay()
    idx = np.arange(C, dtype=np.float64)
    rel = idx[:, None] - idx[None, :]
    dmask = np.where(rel >= 0, np.exp(log_g[:, None, None] * np.maximum(rel, 0.0)), 0.0)
    xi = np.exp(log_g[None, :] * (idx[:, None] + 1.0))
    zeta = np.exp(log_g[None, :] * (C - 1.0 - idx)[:, None])
    spread = lambda t: jnp.asarray(np.repeat(t, RET_HEAD_DIM, axis=1), F32)
    return {"dmask": jnp.asarray(dmask, F32), "xi": spread(xi), "zeta": spread(zeta), "decay_c": _chunk_decay(C)}


def _chunk_decay(C):
    return tuple(math.exp(lg * C) for lg in _log_decay())


def _choose_tile(T, cap):
    tm = min(T, cap)
    while T % tm:
        tm //= 2
    return tm


def _layer_prompt(i, h, p, W, rope, tabs, B, L):
    T = B * L
    tm = _choose_tile(T, 512)
    cb = _choose_tile(L, 256)
    j = i // 2
    z = _in_proj(h, W["norm1"], W["w_in"], i, rope[0], rope[1], _choose_tile(L, tm))
    if i % 2 == 0:
        h1, hn2, pool, ret = _mixer_prompt(z, h, tabs, W, i, B, L, cb)
        h2 = _ffn(hn2, h1, None, W["ffn_g"], W["ffn_u"], W["ffn_d"], j, tm)
        m_t = None
    else:
        h2, xt, meta, cnt, pool, ret = _mixer_prompt(z, h, tabs, W, i, B, L, cb)
        m_t = _moe_sparse(xt, _moe_plan(meta, cnt, B, L, cb), W["moe_g"], W["moe_u"], W["moe_d"], j, B, L)
    return _ple(h2, m_t, p, W, i, tm), pool, ret


def _layer_sample(i, h, p, W, rope, decay_1, buf, s0):
    B = h.shape[0]
    j = i // 2
    z = _in_proj(h, W["norm1"], W["w_in"], i, rope[0], rope[1], B)
    h1, hn2, gates, pool, ret = _mixer_sample(z, h, buf, s0, decay_1, W, i, _choose_tile(B, 16))
    if i % 2 == 0:
        h2 = _ffn(hn2, h1, None, W["ffn_g"], W["ffn_u"], W["ffn_d"], j, B)
    else:
        h2 = _ffn(hn2, h1, gates, W["moe_g"], W["moe_u"], W["moe_d"], j, B)
    return _ple(h2, None, p, W, i, B), pool, ret


def _prepare_weights(norm1, w_in, w_pool, pool_scale, w_o, norm2, ffn_w_gate, ffn_w_up, ffn_w_down,
                     moe_router, moe_w_gate, moe_w_up, moe_w_down, w_ple, w_ple_gate, final_norm):
    r_hi, r_lo = _split_bf16(jnp.pad(moe_router, ((0, 0), (0, 0), (0, LANES - N_EXPERTS))))
    rt_hi, rt_lo = _split_bf16(moe_router.transpose(0, 2, 1))
    return {
        "norm1": norm1[:, None, :], "norm2": norm2[:, None, :], "final_norm": final_norm[None, :],
        "pool_scale": pool_scale[:, None, :],
        "w_in": w_in.astype(BF16), "w_pool": w_pool.astype(BF16), "w_o": w_o.astype(BF16),
        "ffn_g": ffn_w_gate.astype(BF16), "ffn_u": ffn_w_up.astype(BF16), "ffn_d": ffn_w_down.astype(BF16),
        "moe_g": moe_w_gate.astype(BF16), "moe_u": moe_w_up.astype(BF16), "moe_d": moe_w_down.astype(BF16),
        "router_hi": r_hi, "router_lo": r_lo, "router_t_hi": rt_hi, "router_t_lo": rt_lo,
        "w_ple": w_ple.astype(BF16), "w_ple_gate": w_ple_gate.astype(BF16),
    }


def _trunk(x_prompt, x_sample, state_pool, state_ret, p_prompt, p_sample, W):
    B, L, _ = x_prompt.shape
    Bs = x_sample.shape[0]
    hp = x_prompt.reshape(B * L, D_MODEL)
    hs = x_sample.reshape(Bs, D_MODEL)
    pp = p_prompt.reshape(DEPTH, B * L, PLE_DIM)
    ps = p_sample.reshape(DEPTH, Bs, PLE_DIM)
    rope_p = _rope_tables(np.arange(L))
    rope_s = _rope_tables(PAST_LEN + np.arange(1))
    tabs = _retention_tables(_choose_tile(L, 256))
    decay_1 = _chunk_decay(1)
    pool_p, ret_p, pool_s, ret_s = [], [], [], []
    for i in range(DEPTH):
        hp, bp, sp = _layer_prompt(i, hp, pp, W, rope_p, tabs, B, L)
        hs, bs, ss = _layer_sample(i, hs, ps, W, rope_s, decay_1,
                                   state_pool[i].reshape(Bs, POOL_BUF * POOL_WIDTH), state_ret[i])
        pool_p.append(bp)
        ret_p.append(sp)
        pool_s.append(bs.reshape(Bs, POOL_BUF, POOL_WIDTH))
        ret_s.append(ss)
    return (hp.reshape(B, L, D_MODEL), hs.reshape(Bs, 1, D_MODEL),
            jnp.stack(pool_p), jnp.stack(ret_p), jnp.stack(pool_s), jnp.stack(ret_s))


def kernel(x_prompt, x_sample, state_pool, state_ret, p_prompt, p_sample, norm1, w_in, w_pool, pool_scale, w_o, norm2, ffn_w_gate, ffn_w_up, ffn_w_down, moe_router, moe_w_gate, moe_w_up, moe_w_down, w_ple, w_ple_gate, final_norm):
    W = _prepare_weights(norm1, w_in, w_pool, pool_scale, w_o, norm2, ffn_w_gate, ffn_w_up, ffn_w_down,
                         moe_router, moe_w_gate, moe_w_up, moe_w_down, w_ple, w_ple_gate, final_norm)
    return _trunk(x_prompt, x_sample, state_pool, state_ret, p_prompt, p_sample, W)
```

```python
import functools
import math

import numpy as np

import jax
import jax.numpy as jnp
from jax import lax
from jax.experimental import pallas as pl
from jax.experimental.pallas import tpu as pltpu

F32 = jnp.float32
BF16 = jnp.bfloat16
I32 = jnp.int32

D_MODEL = 1024
DEPTH = 4
PAST_LEN = 16384
POOL_WIDTH = 512
POOL_WINDOWS = (2, 4, 8, 16)
POOL_GROUP_DIM = 128
POOL_BUF = 15
RET_HEADS = 8
RET_HEAD_DIM = 64
RET_WIDTH = 512
ROPE_THETA = 10000.0
IN_WIDTH = POOL_WIDTH + 4 * RET_WIDTH
SECTION = 512
D_FF_TILE = 1408
N_EXPERTS = 8
TOP_K = 2
PLE_DIM = 256
NORM_EPS = 1e-6
LANES = 128
SUBLANES = 8
TOKEN_TILES = D_MODEL // LANES
assert TOKEN_TILES == SUBLANES
TAIL_ROWS = 16
MOE_TILE = 256
MOE_FF_CHUNKS = ((0, 512), (512, 1024), (1024, D_FF_TILE))
PAR_SEQS = 1
VMEM_LIMIT = 58 * 1024 * 1024


def _rms(x, g):
    return x * lax.rsqrt(jnp.mean(x * x, axis=-1, keepdims=True) + NORM_EPS) * g


def _dot(a, b):
    return jnp.dot(a, b, preferred_element_type=F32)


def _dot_nt(a, b):
    return lax.dot_general(a, b, (((1,), (1,)), ((), ())), preferred_element_type=F32)


def _dot_tn(a, b):
    return lax.dot_general(a, b, (((0,), (0,)), ((), ())), preferred_element_type=F32)


def _params(*sem):
    return pltpu.CompilerParams(dimension_semantics=sem, vmem_limit_bytes=VMEM_LIMIT)


def _split_bf16(x):
    hi = x.astype(BF16)
    return hi, (x - hi.astype(F32)).astype(BF16)


def _to_token_layout(ref, x):
    n = x.shape[0]
    for j in range(TOKEN_TILES):
        ref[pl.ds(j, n, stride=TOKEN_TILES), :] = x[:, j * LANES:(j + 1) * LANES]


def _from_token_layout(ref, n):
    return jnp.concatenate([ref[pl.ds(j, n, stride=TOKEN_TILES), :] for j in range(TOKEN_TILES)], axis=1)


def _rope_slab(x, cos, sin, first_half):
    fwd = pltpu.roll(x, 32, 1)
    bwd = pltpu.roll(x, LANES - 32, 1)
    return x * cos + jnp.where(first_half, bwd, fwd) * sin


def _in_proj_kernel(h_ref, n1_ref, w_ref, cos_ref, sin_ref, z_ref):
    hn = _rms(h_ref[...], n1_ref[...]).astype(BF16)
    cos = cos_ref[...]
    sin = sin_ref[...]
    lane = lax.broadcasted_iota(I32, (hn.shape[0], LANES), 1)
    first_half = (lane & 32) == 0
    for s in range(IN_WIDTH // SECTION):
        zs = _dot(hn, w_ref[:, s * SECTION:(s + 1) * SECTION])
        if s in (1, 2):
            scale = 1.0 if s == 1 else RET_HEAD_DIM ** -0.5
            for c in range(SECTION // LANES):
                slab = _rope_slab(zs[:, c * LANES:(c + 1) * LANES], cos, sin, first_half)
                z_ref[:, s * SECTION + c * LANES:s * SECTION + (c + 1) * LANES] = slab * scale
        else:
            z_ref[:, s * SECTION:(s + 1) * SECTION] = zs


def _in_proj(h, n1, w_in, i, cos_t, sin_t, tm):
    T = h.shape[0]
    n_tab = cos_t.shape[0] // tm if cos_t.shape[0] > 1 else 1
    tb = tm if cos_t.shape[0] > 1 else 1
    return pl.pallas_call(
        _in_proj_kernel,
        grid=(T // tm,),
        in_specs=[
            pl.BlockSpec((tm, D_MODEL), lambda t: (t, 0)),
            pl.BlockSpec((None, 1, D_MODEL), lambda t: (i, 0, 0)),
            pl.BlockSpec((None, D_MODEL, IN_WIDTH), lambda t: (i, 0, 0)),
            pl.BlockSpec((tb, LANES), lambda t: (t % n_tab, 0)),
            pl.BlockSpec((tb, LANES), lambda t: (t % n_tab, 0)),
        ],
        out_specs=pl.BlockSpec((tm, IN_WIDTH), lambda t: (t, 0)),
        out_shape=jax.ShapeDtypeStruct((T, IN_WIDTH), F32),
        compiler_params=_params("arbitrary"),
        name="in_proj",
    )(h, n1, w_in, cos_t, sin_t)


def _top2(lg, idx, axis, n_idx):
    neg = jnp.float32(-jnp.inf)
    m1 = jnp.max(lg, axis=axis, keepdims=True)
    i1 = jnp.min(jnp.where(lg == m1, idx, n_idx), axis=axis, keepdims=True)
    lg2 = jnp.where(idx == i1, neg, lg)
    m2 = jnp.max(lg2, axis=axis, keepdims=True)
    i2 = jnp.min(jnp.where(lg2 == m2, idx, n_idx), axis=axis, keepdims=True)
    e2 = jnp.exp(m2 - m1)
    den = 1.0 + e2
    return i1, i2, 1.0 / den, e2 / den


def _route(x, rhi_ref, rlo_ref):
    x_hi, x_lo = _split_bf16(x)
    logits = _dot(x_hi, rhi_ref[...]) + _dot(x_lo, rhi_ref[...]) + _dot(x_hi, rlo_ref[...])
    lane = lax.broadcasted_iota(I32, logits.shape, 1)
    lg = jnp.where(lane < N_EXPERTS, logits, jnp.float32(-jnp.inf))
    i1, i2, w1, w2 = _top2(lg, lane, 1, LANES)
    return jnp.where(lane == i1, w1, 0.0) + jnp.where(lane == i2, w2, 0.0)


def _route_sorted(x, rthi_ref, rtlo_ref, carry_ref, meta_ref, cnt_ref):
    n = x.shape[0]
    x_hi, x_lo = _split_bf16(x)
    lt = _dot_nt(rthi_ref[...], x_hi) + _dot_nt(rthi_ref[...], x_lo) + _dot_nt(rtlo_ref[...], x_hi)
    sub = lax.broadcasted_iota(I32, lt.shape, 0)
    i1, i2, w1, w2 = _top2(lt, sub, 0, N_EXPERTS)
    ind = jnp.where(sub == i1, 1.0, 0.0) + jnp.where(sub == i2, 1.0, 0.0)
    tri = jnp.where(lax.broadcasted_iota(I32, (n, n), 0) <= lax.broadcasted_iota(I32, (n, n), 1), 1.0, 0.0)
    csum = _dot(ind.astype(BF16), tri.astype(BF16))
    carry = carry_ref[...]
    rank = carry[:, 0:1] + csum - ind
    r1 = jnp.sum(jnp.where(sub == i1, rank, 0.0), axis=0, keepdims=True).astype(I32)
    r2 = jnp.sum(jnp.where(sub == i2, rank, 0.0), axis=0, keepdims=True).astype(I32)
    carry_ref[...] = carry + csum[:, n - 1:n]
    cnt_ref[0] = carry_ref[...].astype(I32)
    rows = [i1, i2, r1, r2, pltpu.bitcast(w1, I32), pltpu.bitcast(w2, I32)]
    meta = jnp.zeros(lt.shape, I32)
    for j, rowv in enumerate(rows):
        meta = jnp.where(sub == j, jnp.broadcast_to(rowv, lt.shape), meta)
    meta_ref[0] = meta


def _pool_project(d_slabs, u, wpool_ref, pscale_ref):
    ys = []
    for g in range(len(POOL_WINDOWS)):
        sl = slice(g * POOL_GROUP_DIM, (g + 1) * POOL_GROUP_DIM)
        ys.append(_dot((d_slabs[g] - u[:, sl]).astype(BF16), wpool_ref[g]) * pscale_ref[:, sl])
    return ys


def _mixer_prompt_seq(c, seq, shared, cb, decay_c, moe):
    u_ref, q_ref, k_ref, v_ref, g_ref, h_ref = seq[:6]
    dmask_ref, xi_ref, zeta_ref, wpool_ref, pscale_ref, wo_ref, n2_ref = shared[:7]
    if moe:
        rthi_ref, rtlo_ref = shared[7:]
        h1_ref, xt_ref, meta_ref, cnt_ref, pool_out_ref, ret_out_ref, s_scr, p_scr, carry_scr = seq[6:]
    else:
        h1_ref, hn2_ref, pool_out_ref, ret_out_ref, s_scr, p_scr = seq[6:]

    u = u_ref[...]
    p_scr[TAIL_ROWS:TAIL_ROWS + cb, :] = u

    pos = c * cb + lax.broadcasted_iota(I32, (cb, POOL_GROUP_DIM), 0)
    means = []
    for g, w in enumerate(POOL_WINDOWS):
        sl = slice(g * POOL_GROUP_DIM, (g + 1) * POOL_GROUP_DIM)
        s = u[:, sl]
        for j in range(1, w):
            s = s + p_scr[TAIL_ROWS - j:TAIL_ROWS - j + cb, sl]
        means.append(s / jnp.minimum(pos + 1, w).astype(F32))
    ys = _pool_project(means, u, wpool_ref, pscale_ref)
    pool_out_ref[0] = p_scr[cb + TAIL_ROWS - POOL_BUF:cb + TAIL_ROWS, :]
    p_scr[0:TAIL_ROWS, :] = p_scr[cb:cb + TAIL_ROWS, :]

    q = q_ref[...]
    k = k_ref[...]
    qb = q.astype(BF16)
    kb = k.astype(BF16)
    kz = (k * zeta_ref[...]).astype(BF16)
    vb = v_ref[...].astype(BF16)
    xi = xi_ref[...]
    heads = range(RET_HEADS)
    sls = [slice(hd * RET_HEAD_DIM, (hd + 1) * RET_HEAD_DIM) for hd in heads]
    s_prev = [s_scr[hd] for hd in heads]
    sc = [_dot_nt(qb[:, sl], kb[:, sl]) for sl in sls]
    cross = [_dot(qb[:, sl], s_prev[hd].astype(BF16)) for hd, sl in enumerate(sls)]
    upd = [_dot_tn(kz[:, sl], vb[:, sl]) for sl in sls]
    pm = [(sc[hd] * dmask_ref[hd]).astype(BF16) for hd in heads]
    o_h = [_dot(pm[hd], vb[:, sl]) + cross[hd] * xi[:, sl] for hd, sl in enumerate(sls)]
    for hd in heads:
        s_scr[hd] = s_prev[hd] * decay_c[hd] + upd[hd]
    mu = [jnp.mean(o, axis=-1, keepdims=True) for o in o_h]
    cen = [o - m for o, m in zip(o_h, mu)]
    var = [jnp.mean(jnp.square(x), axis=-1, keepdims=True) for x in cen]
    os_ = [x * lax.rsqrt(v + NORM_EPS) for x, v in zip(cen, var)]
    ret_out_ref[0] = s_scr[...]
    o = jnp.concatenate(os_, axis=1) * jax.nn.silu(g_ref[...])
    mix = jnp.concatenate(ys + [o], axis=1)

    h1 = h_ref[...] + _dot(mix.astype(BF16), wo_ref[...])
    h1_ref[...] = h1
    hn2 = _rms(h1, n2_ref[...])
    if moe:
        _to_token_layout(xt_ref, hn2)
        _route_sorted(hn2, rthi_ref, rtlo_ref, carry_scr, meta_ref, cnt_ref)
    else:
        hn2_ref[...] = hn2.astype(BF16)


def _mixer_prompt_kernel(*refs, cb, decay_c, moe, n_par):
    n_shared = 9 if moe else 7
    seq_in, shared, seq_rest = refs[:6], refs[6:6 + n_shared], refs[6 + n_shared:]
    c = pl.program_id(1)
    scratch = seq_rest[-3:] if moe else seq_rest[-2:]

    @pl.when(c == 0)
    def _():
        s_scr, p_scr = scratch[:2]
        s_scr[...] = jnp.zeros_like(s_scr)
        p_scr[:, 0:TAIL_ROWS, :] = jnp.zeros((n_par, TAIL_ROWS, POOL_WIDTH), F32)
        if moe:
            scratch[2][...] = jnp.zeros_like(scratch[2])

    for p in range(n_par):
        seq = [r.at[p] for r in seq_in] + [r.at[p] for r in seq_rest]
        _mixer_prompt_seq(c, seq, shared, cb, decay_c, moe)


def _mixer_prompt(z, h, tabs, W, i, B, L, cb):
    T = B * L
    nc = L // cb
    moe = i % 2 == 1
    npar = PAR_SEQS if B % PAR_SEQS == 0 else 1
    bg = B // npar
    row = lambda b, c: b * nc + c
    zspec = lambda s: pl.BlockSpec((npar, cb, SECTION), lambda b, c, s=s: (0, row(b, c), s))
    const2 = lambda shape: pl.BlockSpec(shape, lambda b, c: (0, 0))
    const3 = lambda shape: pl.BlockSpec(shape, lambda b, c: (0, 0, 0))
    layer3 = lambda shape: pl.BlockSpec((None,) + shape, lambda b, c: (i, 0, 0))
    rows = pl.BlockSpec((npar, cb, D_MODEL), lambda b, c: (0, row(b, c), 0))
    zp = z.reshape(npar, T // npar, IN_WIDTH)
    in_specs = [zspec(0), zspec(1), zspec(2), zspec(3), zspec(4), rows,
                const3((RET_HEADS, cb, cb)), const2((cb, RET_WIDTH)), const2((cb, RET_WIDTH)),
                pl.BlockSpec((None, len(POOL_WINDOWS), POOL_GROUP_DIM, POOL_GROUP_DIM), lambda b, c: (i, 0, 0, 0)),
                layer3((1, POOL_WIDTH)), layer3((D_MODEL, D_MODEL)), layer3((1, D_MODEL))]
    args = [zp, zp, zp, zp, zp, h.reshape(npar, T // npar, D_MODEL), tabs["dmask"], tabs["xi"], tabs["zeta"],
            W["w_pool"], W["pool_scale"], W["w_o"], W["norm2"]]
    scratch = [pltpu.VMEM((npar, RET_HEADS, RET_HEAD_DIM, RET_HEAD_DIM), F32),
               pltpu.VMEM((npar, TAIL_ROWS + cb, POOL_WIDTH), F32)]
    state_specs = [pl.BlockSpec((npar, 1, POOL_BUF, POOL_WIDTH), lambda b, c: (0, b, 0, 0)),
                   pl.BlockSpec((npar, 1, RET_HEADS, RET_HEAD_DIM, RET_HEAD_DIM), lambda b, c: (0, b, 0, 0, 0))]
    state_shapes = [jax.ShapeDtypeStruct((npar, bg, POOL_BUF, POOL_WIDTH), F32),
                    jax.ShapeDtypeStruct((npar, bg, RET_HEADS, RET_HEAD_DIM, RET_HEAD_DIM), F32)]
    if moe:
        j = i // 2
        rt = pl.BlockSpec((None, N_EXPERTS, D_MODEL), lambda b, c: (j, 0, 0))
        in_specs += [rt, rt]
        args += [W["router_t_hi"], W["router_t_lo"]]
        out_specs = [rows,
                     pl.BlockSpec((npar, cb * TOKEN_TILES, LANES), lambda b, c: (0, row(b, c), 0)),
                     pl.BlockSpec((npar, 1, SUBLANES, cb), lambda b, c: (0, row(b, c), 0, 0)),
                     pl.BlockSpec((npar, 1, N_EXPERTS, LANES), lambda b, c: (0, b, 0, 0))]
        out_shape = [jax.ShapeDtypeStruct((npar, T // npar, D_MODEL), F32),
                     jax.ShapeDtypeStruct((npar, T // npar * TOKEN_TILES, LANES), F32),
                     jax.ShapeDtypeStruct((npar, bg * nc, SUBLANES, cb), I32),
                     jax.ShapeDtypeStruct((npar, bg, N_EXPERTS, LANES), I32)]
        scratch.append(pltpu.VMEM((npar, N_EXPERTS, LANES), F32))
    else:
        out_specs = [rows, rows]
        out_shape = [jax.ShapeDtypeStruct((npar, T // npar, D_MODEL), F32),
                     jax.ShapeDtypeStruct((npar, T // npar, D_MODEL), BF16)]
    outs = pl.pallas_call(
        functools.partial(_mixer_prompt_kernel, cb=cb, decay_c=tabs["decay_c"], moe=moe, n_par=npar),
        grid=(bg, nc),
        in_specs=in_specs,
        out_specs=out_specs + state_specs,
        out_shape=out_shape + state_shapes,
        scratch_shapes=scratch,
        compiler_params=_params("arbitrary", "arbitrary"),
        name="mixer_prompt",
    )(*args)
    return [o.reshape((o.shape[0] * o.shape[1],) + o.shape[2:]) for o in outs]


def _to_head_rows(ref, x):
    n = x.shape[0]
    for hd in range(RET_HEADS):
        ref[pl.ds(hd, n, stride=RET_HEADS), 0:RET_HEAD_DIM] = x[:, hd * RET_HEAD_DIM:(hd + 1) * RET_HEAD_DIM]


def _mixer_sample_kernel(*refs, bb, decay_1, moe, n_alias):
    (u_ref, q_ref, k_ref, v_ref, g_ref, h_ref, buf_ref, s_ref,
     wpool_ref, pscale_ref, wo_ref, n2_ref) = refs[:12]
    n_routed = 14 if moe else 12
    router_refs = refs[12:n_routed]
    n_in = n_routed + n_alias
    h1_ref, hn2_ref = refs[n_in:n_in + 2]
    gates_ref = refs[n_in + 2] if moe else None
    buf_out_ref, s_out_ref, q2_scr, k2_scr, v2_scr, g2_scr, c2_scr, o2_scr, qrep_scr, krep_scr = refs[-10:]
    dh = RET_HEAD_DIM
    grp = SUBLANES

    u = u_ref[...]
    means = []
    for g, w in enumerate(POOL_WINDOWS):
        s = u[:, g * POOL_GROUP_DIM:(g + 1) * POOL_GROUP_DIM]
        for j in range(1, w):
            lo = (POOL_BUF - j) * POOL_WIDTH + g * POOL_GROUP_DIM
            s = s + buf_ref[:, lo:lo + POOL_GROUP_DIM]
        means.append(s / float(min(PAST_LEN + 1, w)))
    ys = _pool_project(means, u, wpool_ref, pscale_ref)
    keep = (POOL_BUF - 1) * POOL_WIDTH
    buf_out_ref[:, 0:keep] = buf_ref[:, POOL_WIDTH:POOL_WIDTH + keep]
    buf_out_ref[:, keep:keep + POOL_WIDTH] = u

    for ref, src in ((q2_scr, q_ref), (k2_scr, k_ref), (v2_scr, v_ref), (g2_scr, g_ref)):
        _to_head_rows(ref, src[...])
    spread = jnp.where(lax.broadcasted_iota(I32, (grp, grp * LANES), 0)
                       == lax.broadcasted_iota(I32, (grp, grp * LANES), 1) // LANES, 1.0, 0.0)

    def per_group(gi, carry):
        rows = pl.ds(pl.multiple_of(gi * grp, grp), grp)
        qrep_scr[...] = _dot_tn(q_ref[rows, :], spread)
        krep_scr[...] = _dot_tn(k_ref[rows, :], spread)
        for bl in range(grp):
            b = gi * grp + bl
            cross = []
            for hd in range(RET_HEADS):
                tile = (slice(hd * dh, (hd + 1) * dh), slice(bl * LANES, bl * LANES + dh))
                s_prev = s_ref[b, hd]
                cross.append(jnp.sum(s_prev * qrep_scr[tile], axis=0, keepdims=True))
                v_row = v2_scr[pl.ds(b * RET_HEADS + hd, 1), 0:dh]
                s_out_ref[b, hd] = s_prev * decay_1[hd] + krep_scr[tile] * v_row
            c2_scr[pl.ds(pl.multiple_of(b * RET_HEADS, RET_HEADS), RET_HEADS), 0:dh] = jnp.concatenate(cross, axis=0)
        return carry

    lax.fori_loop(0, bb // grp, per_group, 0)

    n2 = bb * RET_HEADS
    q2, k2, v2 = q2_scr[:, 0:dh], k2_scr[:, 0:dh], v2_scr[:, 0:dh]
    head = lax.broadcasted_iota(I32, (n2, dh), 0) % RET_HEADS
    gamma = jnp.zeros((n2, dh), F32)
    for hd in range(RET_HEADS):
        gamma = jnp.where(head == hd, decay_1[hd], gamma)
    o2 = jnp.sum(q2 * k2, axis=-1, keepdims=True) * v2 + gamma * c2_scr[:, 0:dh]
    mu = jnp.mean(o2, axis=-1, keepdims=True)
    var = jnp.mean(jnp.square(o2 - mu), axis=-1, keepdims=True)
    o2_scr[:, 0:dh] = (o2 - mu) * lax.rsqrt(var + NORM_EPS) * jax.nn.silu(g2_scr[:, 0:dh])
    o = [o2_scr[pl.ds(hd, bb, stride=RET_HEADS), 0:dh] for hd in range(RET_HEADS)]
    mix = jnp.concatenate(ys + o, axis=1)
    h1 = h_ref[...] + _dot(mix.astype(BF16), wo_ref[...])
    h1_ref[...] = h1
    hn2 = _rms(h1, n2_ref[...])
    hn2_ref[...] = hn2.astype(BF16)
    if moe:
        gates_ref[...] = _route(hn2, *router_refs)


def _mixer_sample(z, h, bufs, states, stacked, decay_1, W, i, bb):
    B = h.shape[0]
    moe = i % 2 == 1
    flat = POOL_BUF * POOL_WIDTH
    zspec = lambda s: pl.BlockSpec((bb, SECTION), lambda t, s=s: (t, s))
    layer3 = lambda shape: pl.BlockSpec((None,) + shape, lambda t: (i, 0, 0))
    rows = pl.BlockSpec((bb, D_MODEL), lambda t: (t, 0))
    pool = pl.BlockSpec((None, bb, flat), lambda t: (i, t, 0))
    state = pl.BlockSpec((None, bb, RET_HEADS, RET_HEAD_DIM, RET_HEAD_DIM), lambda t: (i, t, 0, 0, 0))
    in_specs = [zspec(0), zspec(1), zspec(2), zspec(3), zspec(4), rows, pool, state,
                pl.BlockSpec((None, len(POOL_WINDOWS), POOL_GROUP_DIM, POOL_GROUP_DIM), lambda t: (i, 0, 0, 0)),
                layer3((1, POOL_WIDTH)), layer3((D_MODEL, D_MODEL)), layer3((1, D_MODEL))]
    args = [z, z, z, z, z, h, bufs, states, W["w_pool"], W["pool_scale"], W["w_o"], W["norm2"]]
    out_specs = [rows, rows]
    out_shape = [jax.ShapeDtypeStruct((B, D_MODEL), F32), jax.ShapeDtypeStruct((B, D_MODEL), BF16)]
    if moe:
        j = i // 2
        r = pl.BlockSpec((None, D_MODEL, LANES), lambda t: (j, 0, 0))
        in_specs += [r, r]
        args += [W["router_hi"], W["router_lo"]]
        out_specs.append(pl.BlockSpec((bb, LANES), lambda t: (t, 0)))
        out_shape.append(jax.ShapeDtypeStruct((B, LANES), F32))
    aliases = {}
    if stacked is not None:
        for a in stacked:
            aliases[len(args)] = len(out_shape) + len(aliases)
            in_specs.append(pl.BlockSpec(memory_space=pl.ANY))
            args.append(a)
    out_specs += [pool, state]
    out_shape += [jax.ShapeDtypeStruct(bufs.shape, F32), jax.ShapeDtypeStruct(states.shape, F32)]
    outs = pl.pallas_call(
        functools.partial(_mixer_sample_kernel, bb=bb, decay_1=decay_1, moe=moe, n_alias=len(aliases)),
        grid=(B // bb,),
        in_specs=in_specs,
        out_specs=out_specs,
        out_shape=out_shape,
        input_output_aliases=aliases,
        scratch_shapes=[pltpu.VMEM((bb * RET_HEADS, LANES), F32) for _ in range(6)]
        + [pltpu.VMEM((RET_WIDTH, SUBLANES * LANES), F32) for _ in range(2)],
        compiler_params=_params("arbitrary"),
        name="mixer_sample",
    )(*args)
    if moe:
        h1, hn2, gates, pool_out, ret_out = outs
    else:
        (h1, hn2, pool_out, ret_out), gates = outs, None
    return h1, hn2, gates, (pool_out, ret_out)


def _swiglu(x, wg_ref, wu_ref, wd_ref, between_chunks=None):
    y = None
    for c, (lo, hi) in enumerate(MOE_FF_CHUNKS):
        if between_chunks is not None:
            between_chunks(c)
        hidden = jax.nn.silu(_dot(x, wg_ref[:, lo:hi])) * _dot(x, wu_ref[:, lo:hi])
        part = _dot(hidden.astype(BF16), wd_ref[lo:hi, :])
        y = part if y is None else y + part
    return y


def _ffn_kernel(*refs, moe):
    if moe:
        x_ref, h_ref, gates_ref, wg_ref, wu_ref, wd_ref, out_ref, acc_ref = refs
    else:
        x_ref, h_ref, wg_ref, wu_ref, wd_ref, out_ref, acc_ref = refs
    f = pl.program_id(1)
    y = _swiglu(x_ref[...], wg_ref, wu_ref, wd_ref)
    if moe:
        gates = gates_ref[...]
        lane = lax.broadcasted_iota(I32, gates.shape, 1)
        y = y * jnp.sum(jnp.where(lane == f, gates, 0.0), axis=-1, keepdims=True)

    @pl.when(f == 0)
    def _():
        acc_ref[...] = y

    @pl.when(f > 0)
    def _():
        acc_ref[...] += y

    @pl.when(f == pl.num_programs(1) - 1)
    def _():
        out_ref[...] = h_ref[...] + acc_ref[...]


def _ffn(x, h, gates, wg, wu, wd, j, tm):
    T = x.shape[0]
    moe = gates is not None
    tf = D_FF_TILE
    row = pl.BlockSpec((tm, D_MODEL), lambda t, f: (t, 0))
    in_specs = [row, row]
    args = [x, h]
    if moe:
        nf = wg.shape[1]
        in_specs.append(pl.BlockSpec((tm, LANES), lambda t, f: (t, 0)))
        args.append(gates)
        in_specs += [pl.BlockSpec((None, None, D_MODEL, tf), lambda t, f: (j, f, 0, 0)),
                     pl.BlockSpec((None, None, D_MODEL, tf), lambda t, f: (j, f, 0, 0)),
                     pl.BlockSpec((None, None, tf, D_MODEL), lambda t, f: (j, f, 0, 0))]
    else:
        nf = wg.shape[2] // tf
        in_specs += [pl.BlockSpec((None, D_MODEL, tf), lambda t, f: (j, 0, f)),
                     pl.BlockSpec((None, D_MODEL, tf), lambda t, f: (j, 0, f)),
                     pl.BlockSpec((None, tf, D_MODEL), lambda t, f: (j, f, 0))]
    args += [wg, wu, wd]
    return pl.pallas_call(
        functools.partial(_ffn_kernel, moe=moe),
        grid=(T // tm, nf),
        in_specs=in_specs,
        out_specs=row,
        out_shape=jax.ShapeDtypeStruct((T, D_MODEL), F32),
        scratch_shapes=[pltpu.VMEM((tm, D_MODEL), F32)],
        compiler_params=_params("arbitrary", "arbitrary"),
        name="ffn_moe" if moe else "ffn_dense",
    )(*args)


def _moe_sparse_kernel(dest_ref, w1_ref, w2_ref, tile_e_ref, tile_rows_ref, n_valid_ref,
                       xt_ref, wg_ref, wu_ref, wd_ref, out_ref,
                       src_ref, ws_ref, g_ref, y_ref, *, tb):
    b = pl.program_id(0)
    j = pl.program_id(1)
    n_steps = pl.num_programs(1)
    n_valid = n_valid_ref[b]
    ts = MOE_TILE
    unroll = SUBLANES
    tile8 = lambda r: pl.ds(pl.multiple_of(r * TOKEN_TILES, TOKEN_TILES), TOKEN_TILES)

    rows8 = lambda start: pl.ds(pl.multiple_of(start, TOKEN_TILES), TOKEN_TILES)
    slot_rows = tb * TOKEN_TILES

    def gather_row(tile, buf, r):
        g_ref[buf, tile8(r), :] = xt_ref[rows8(src_ref[tile * ts + r] & (slot_rows - 1)), :]

    def scatter_row(tile, buf, r):
        out_ref[rows8(src_ref[tile * ts + r]), :] = ws_ref[tile * ts + r] * y_ref[buf, tile8(r), :]

    def row_loop(row_fn, tile, buf):
        def group(g, carry):
            for u in range(unroll):
                row_fn(tile, buf, g * unroll + u)
            return carry
        lax.fori_loop(0, ts // unroll, group, 0)

    @pl.when((b == 0) & (j == 0))
    def _():
        g_ref[...] = jnp.zeros_like(g_ref)
        y_ref[...] = jnp.zeros_like(y_ref)

    @pl.when(j == 0)
    def _():
        def fill(t, carry):
            d = dest_ref[b * tb + t]
            d1 = d & 0xFFFF
            d2 = d >> 16
            src_ref[d1] = t * TOKEN_TILES
            src_ref[d2] = t * TOKEN_TILES + slot_rows
            ws_ref[d1] = w1_ref[b * tb + t]
            ws_ref[d2] = w2_ref[b * tb + t]
            return carry

        lax.fori_loop(0, tb, fill, 0, unroll=unroll)

        def pad_tile(tile, carry):
            n_rows = tile_rows_ref[b * n_steps + tile]
            last = tile * ts + n_rows - 1

            def pad_row(r, c):
                src_ref[tile * ts + r] = src_ref[last]
                ws_ref[tile * ts + r] = ws_ref[last]
                return c

            return lax.fori_loop(n_rows, ts, pad_row, carry)

        lax.fori_loop(0, n_valid, pad_tile, 0)
        row_loop(gather_row, 0, 0)

    @pl.when(j < n_valid)
    def _():
        cur = j & 1
        nxt = 1 - cur
        tile_next = jnp.minimum(j + 1, n_valid - 1)
        tile_prev = jnp.maximum(j - 1, 0)
        x = _from_token_layout(g_ref.at[cur], ts).astype(BF16)
        n_chunks = len(MOE_FF_CHUNKS)

        def move_rows(c):
            for r in range(c * ts // n_chunks, (c + 1) * ts // n_chunks):
                gather_row(tile_next, nxt, r)
                scatter_row(tile_prev, nxt, r)

        _to_token_layout(y_ref.at[cur], _swiglu(x, wg_ref, wu_ref, wd_ref, move_rows))

    @pl.when(j == n_valid)
    def _():
        row_loop(scatter_row, n_valid - 1, (n_valid - 1) & 1)


def _moe_plan(meta, cnt, B, L, cb):
    ts = MOE_TILE
    n_steps = TOP_K * L // ts + N_EXPERTS + 1
    m = meta.reshape(B, L // cb, SUBLANES, cb).transpose(2, 0, 1, 3).reshape(SUBLANES, B, L)
    e1, e2, r1, r2 = m[0], m[1], m[2], m[3]
    w1 = lax.bitcast_convert_type(m[4], F32).reshape(B * L)
    w2 = lax.bitcast_convert_type(m[5], F32).reshape(B * L)
    cnt = cnt[:, :, 0]
    padded = (cnt + ts - 1) // ts * ts
    ends = jnp.cumsum(padded, axis=1)
    off = ends - padded
    experts = jnp.arange(N_EXPERTS, dtype=I32)

    def pick(table, idx):
        return jnp.sum(jnp.where(idx[:, :, None] == experts, table[:, None, :], 0), axis=2)

    d1 = pick(off, e1) + r1
    d2 = pick(off, e2) + r2
    dest = (d1 | (d2 << 16)).reshape(B * L).astype(I32)
    start = jnp.arange(n_steps, dtype=I32) * ts
    tile_e = jnp.sum((ends[:, None, :] <= start[None, :, None]).astype(I32), axis=2)
    valid = start[None, :] < ends[:, -1:]
    last_e = jnp.max(jnp.where(valid, tile_e, 0), axis=1, keepdims=True)
    tile_e = jnp.where(valid, tile_e, last_e)
    filled = pick(cnt, tile_e) - (start[None, :] - pick(off, tile_e))
    tile_rows = jnp.where(valid, jnp.clip(filled, 0, ts), 0)
    n_valid = (ends[:, -1] // ts).astype(I32)
    return dest, w1, w2, tile_e.reshape(-1).astype(I32), tile_rows.reshape(-1).astype(I32), n_valid, n_steps


def _moe_sparse(xt, plan, wg, wu, wd, j, B, L):
    dest, w1, w2, tile_e, tile_rows, n_valid, n_steps = plan
    ts = MOE_TILE
    tf = D_FF_TILE
    slot_rows = L * TOKEN_TILES
    assert slot_rows & (slot_rows - 1) == 0, "the kernel masks row offsets with slot_rows - 1"
    tok = pl.BlockSpec((slot_rows, LANES), lambda b, k, *_: (b, 0), pipeline_mode=pl.Buffered(1))
    tok_out = pl.BlockSpec((TOP_K * slot_rows, LANES), lambda b, k, *_: (b, 0), pipeline_mode=pl.Buffered(1))
    expert = lambda b, k, dest, w1, w2, te, tr, nv: (j, te[b * n_steps + k], 0, 0)
    grid_spec = pltpu.PrefetchScalarGridSpec(
        num_scalar_prefetch=6,
        grid=(B, n_steps),
        in_specs=[tok,
                  pl.BlockSpec((None, None, D_MODEL, tf), expert),
                  pl.BlockSpec((None, None, D_MODEL, tf), expert),
                  pl.BlockSpec((None, None, tf, D_MODEL), expert)],
        out_specs=tok_out,
        scratch_shapes=[pltpu.SMEM((n_steps * ts,), I32), pltpu.SMEM((n_steps * ts,), F32),
                        pltpu.VMEM((2, ts * TOKEN_TILES, LANES), F32), pltpu.VMEM((2, ts * TOKEN_TILES, LANES), F32)],
    )
    out = pl.pallas_call(
        functools.partial(_moe_sparse_kernel, tb=L),
        grid_spec=grid_spec,
        out_shape=jax.ShapeDtypeStruct((B * TOP_K * slot_rows, LANES), F32),
        compiler_params=_params("arbitrary", "arbitrary"),
        name="moe_sparse",
    )(dest, w1, w2, tile_e, tile_rows, n_valid, xt, wg, wu, wd)
    return out.reshape(B, TOP_K, slot_rows, LANES)


def _ple_kernel(*refs, final, add_tokens):
    refs = list(refs)
    h_ref = refs.pop(0)
    m_ref = refs.pop(0) if add_tokens else None
    p_ref, wg_ref, wp_ref = refs[:3]
    fn_ref = refs[3] if final else None
    out_ref = refs[-1]
    h = h_ref[...]
    if add_tokens:
        for s in range(TOP_K):
            h = h + _from_token_layout(m_ref.at[s], h.shape[0])
    gate = jax.nn.sigmoid(_dot(h.astype(BF16), wg_ref[...]))
    e = _dot(p_ref[...].astype(BF16), wp_ref[...])
    h3 = h + gate * e
    out_ref[...] = _rms(h3, fn_ref[...]) if final else h3


def _ple(h, m_t, p, W, i, tm):
    T = h.shape[0]
    final = i == DEPTH - 1
    add_tokens = m_t is not None
    rows = pl.BlockSpec((tm, D_MODEL), lambda t: (t, 0))
    in_specs = [rows]
    args = [h]
    if add_tokens:
        per_seq = m_t.shape[2] // (tm * TOKEN_TILES)
        in_specs.append(pl.BlockSpec((None, TOP_K, tm * TOKEN_TILES, LANES),
                                     lambda t: (t // per_seq, 0, t % per_seq, 0)))
        args.append(m_t)
    in_specs += [pl.BlockSpec((None, tm, PLE_DIM), lambda t: (i, t, 0)),
                 pl.BlockSpec((None, D_MODEL, D_MODEL), lambda t: (i, 0, 0)),
                 pl.BlockSpec((None, PLE_DIM, D_MODEL), lambda t: (i, 0, 0))]
    args += [p, W["w_ple_gate"], W["w_ple"]]
    if final:
        in_specs.append(pl.BlockSpec((1, D_MODEL), lambda t: (0, 0)))
        args.append(W["final_norm"])
    return pl.pallas_call(
        functools.partial(_ple_kernel, final=final, add_tokens=add_tokens),
        grid=(T // tm,),
        in_specs=in_specs,
        out_specs=rows,
        out_shape=jax.ShapeDtypeStruct((T, D_MODEL), F32),
        compiler_params=_params("arbitrary"),
        name="ple_final" if final else "ple",
    )(*args)


def _rope_tables(pos):
    half = RET_HEAD_DIM // 2
    inv = ROPE_THETA ** (-np.arange(half, dtype=np.float64) / half)
    ang = np.asarray(pos, np.float64)[:, None] * inv[None, :]
    cos, sin = np.cos(ang), np.sin(ang)
    reps = LANES // RET_HEAD_DIM
    return (jnp.asarray(np.tile(np.concatenate([cos, cos], axis=-1), (1, reps)), F32),
            jnp.asarray(np.tile(np.concatenate([-sin, sin], axis=-1), (1, reps)), F32))


def _log_decay():
    return np.log1p(-np.power(2.0, -5.0 - np.arange(RET_HEADS, dtype=np.float64)))


def _retention_tables(C):
    log_g = _log_decay()
    idx = np.arange(C, dtype=np.float64)
    rel = idx[:, None] - idx[None, :]
    dmask = np.where(rel >= 0, np.exp(log_g[:, None, None] * np.maximum(rel, 0.0)), 0.0)
    xi = np.exp(log_g[None, :] * (idx[:, None] + 1.0))
    zeta = np.exp(log_g[None, :] * (C - 1.0 - idx)[:, None])
    spread = lambda t: jnp.asarray(np.repeat(t, RET_HEAD_DIM, axis=1), F32)
    return {"dmask": jnp.asarray(dmask, F32), "xi": spread(xi), "zeta": spread(zeta), "decay_c": _chunk_decay(C)}


def _chunk_decay(C):
    return tuple(math.exp(lg * C) for lg in _log_decay())


def _choose_tile(T, cap):
    tm = min(T, cap)
    while T % tm:
        tm //= 2
    return tm


def _layer_prompt(i, h, p, W, rope, tabs, B, L):
    T = B * L
    tm = _choose_tile(T, 512)
    cb = _choose_tile(L, 256)
    j = i // 2
    z = _in_proj(h, W["norm1"], W["w_in"], i, rope[0], rope[1], _choose_tile(L, tm))
    if i % 2 == 0:
        h1, hn2, pool, ret = _mixer_prompt(z, h, tabs, W, i, B, L, cb)
        h2 = _ffn(hn2, h1, None, W["ffn_g"], W["ffn_u"], W["ffn_d"], j, tm)
        m_t = None
    else:
        h2, xt, meta, cnt, pool, ret = _mixer_prompt(z, h, tabs, W, i, B, L, cb)
        m_t = _moe_sparse(xt, _moe_plan(meta, cnt, B, L, cb), W["moe_g"], W["moe_u"], W["moe_d"], j, B, L)
    return _ple(h2, m_t, p, W, i, tm), pool, ret


def _layer_sample(i, h, p, W, rope, decay_1, bufs, states, stacked):
    B = h.shape[0]
    j = i // 2
    z = _in_proj(h, W["norm1"], W["w_in"], i, rope[0], rope[1], B)
    h1, hn2, gates, stacked = _mixer_sample(z, h, bufs, states, stacked, decay_1, W, i, _choose_tile(B, 16))
    if i % 2 == 0:
        h2 = _ffn(hn2, h1, None, W["ffn_g"], W["ffn_u"], W["ffn_d"], j, B)
    else:
        h2 = _ffn(hn2, h1, gates, W["moe_g"], W["moe_u"], W["moe_d"], j, B)
    return _ple(h2, None, p, W, i, B), stacked


def _prepare_weights(norm1, w_in, w_pool, pool_scale, w_o, norm2, ffn_w_gate, ffn_w_up, ffn_w_down,
                     moe_router, moe_w_gate, moe_w_up, moe_w_down, w_ple, w_ple_gate, final_norm):
    r_hi, r_lo = _split_bf16(jnp.pad(moe_router, ((0, 0), (0, 0), (0, LANES - N_EXPERTS))))
    rt_hi, rt_lo = _split_bf16(moe_router.transpose(0, 2, 1))
    return {
        "norm1": norm1[:, None, :], "norm2": norm2[:, None, :], "final_norm": final_norm[None, :],
        "pool_scale": pool_scale[:, None, :],
        "w_in": w_in.astype(BF16), "w_pool": w_pool.astype(BF16), "w_o": w_o.astype(BF16),
        "ffn_g": ffn_w_gate.astype(BF16), "ffn_u": ffn_w_up.astype(BF16), "ffn_d": ffn_w_down.astype(BF16),
        "moe_g": moe_w_gate.astype(BF16), "moe_u": moe_w_up.astype(BF16), "moe_d": moe_w_down.astype(BF16),
        "router_hi": r_hi, "router_lo": r_lo, "router_t_hi": rt_hi, "router_t_lo": rt_lo,
        "w_ple": w_ple.astype(BF16), "w_ple_gate": w_ple_gate.astype(BF16),
    }


def _trunk(x_prompt, x_sample, state_pool, state_ret, p_prompt, p_sample, W):
    B, L, _ = x_prompt.shape
    Bs = x_sample.shape[0]
    hp = x_prompt.reshape(B * L, D_MODEL)
    hs = x_sample.reshape(Bs, D_MODEL)
    pp = p_prompt.reshape(DEPTH, B * L, PLE_DIM)
    ps = p_sample.reshape(DEPTH, Bs, PLE_DIM)
    rope_p = _rope_tables(np.arange(L))
    rope_s = _rope_tables(PAST_LEN + np.arange(1))
    tabs = _retention_tables(_choose_tile(L, 256))
    decay_1 = _chunk_decay(1)
    pool_p, ret_p = [], []
    bufs_s = state_pool.reshape(DEPTH, Bs, POOL_BUF * POOL_WIDTH)
    stacked_s = None
    for i in range(DEPTH):
        hp, bp, sp = _layer_prompt(i, hp, pp, W, rope_p, tabs, B, L)
        hs, stacked_s = _layer_sample(i, hs, ps, W, rope_s, decay_1, bufs_s, state_ret, stacked_s)
        pool_p.append(bp)
        ret_p.append(sp)
    return (hp.reshape(B, L, D_MODEL), hs.reshape(Bs, 1, D_MODEL), jnp.stack(pool_p), jnp.stack(ret_p),
            stacked_s[0].reshape(DEPTH, Bs, POOL_BUF, POOL_WIDTH), stacked_s[1])


def kernel(x_prompt, x_sample, state_pool, state_ret, p_prompt, p_sample, norm1, w_in, w_pool, pool_scale, w_o, norm2, ffn_w_gate, ffn_w_up, ffn_w_down, moe_router, moe_w_gate, moe_w_up, moe_w_down, w_ple, w_ple_gate, final_norm):
    W = _prepare_weights(norm1, w_in, w_pool, pool_scale, w_o, norm2, ffn_w_gate, ffn_w_up, ffn_w_down,
                         moe_router, moe_w_gate, moe_w_up, moe_w_down, w_ple, w_ple_gate, final_norm)
    return _trunk(x_prompt, x_sample, state_pool, state_ret, p_prompt, p_sample, W)
```

```python
import functools
import math

import numpy as np

import jax
import jax.numpy as jnp
from jax import lax
from jax.experimental import pallas as pl
from jax.experimental.pallas import tpu as pltpu

F32 = jnp.float32
BF16 = jnp.bfloat16
I32 = jnp.int32

D_MODEL = 1024
DEPTH = 4
PAST_LEN = 16384
POOL_WIDTH = 512
POOL_WINDOWS = (2, 4, 8, 16)
POOL_GROUP_DIM = 128
POOL_BUF = 15
RET_HEADS = 8
RET_HEAD_DIM = 64
RET_WIDTH = 512
ROPE_THETA = 10000.0
IN_WIDTH = POOL_WIDTH + 4 * RET_WIDTH
SECTION = 512
D_FF_TILE = 1408
N_EXPERTS = 8
TOP_K = 2
PLE_DIM = 256
NORM_EPS = 1e-6
LANES = 128
SUBLANES = 8
TOKEN_TILES = D_MODEL // LANES
assert TOKEN_TILES == SUBLANES
TAIL_ROWS = 16
MOE_TILE = 256
MOE_FF_CHUNKS = ((0, 512), (512, 1024), (1024, D_FF_TILE))
VMEM_LIMIT = 58 * 1024 * 1024


def _rms(x, g):
    return x * lax.rsqrt(jnp.mean(x * x, axis=-1, keepdims=True) + NORM_EPS) * g


def _dot(a, b):
    return jnp.dot(a, b, preferred_element_type=F32)


def _dot_nt(a, b):
    return lax.dot_general(a, b, (((1,), (1,)), ((), ())), preferred_element_type=F32)


def _dot_tn(a, b):
    return lax.dot_general(a, b, (((0,), (0,)), ((), ())), preferred_element_type=F32)


def _params(*sem):
    return pltpu.CompilerParams(dimension_semantics=sem, vmem_limit_bytes=VMEM_LIMIT)


def _split_bf16(x):
    hi = x.astype(BF16)
    return hi, (x - hi.astype(F32)).astype(BF16)


def _to_token_layout(ref, x):
    n = x.shape[0]
    for j in range(TOKEN_TILES):
        ref[pl.ds(j, n, stride=TOKEN_TILES), :] = x[:, j * LANES:(j + 1) * LANES]


def _from_token_layout(ref, n):
    return jnp.concatenate([ref[pl.ds(j, n, stride=TOKEN_TILES), :] for j in range(TOKEN_TILES)], axis=1)


def _rope_slab(x, cos, sin, first_half):
    fwd = pltpu.roll(x, 32, 1)
    bwd = pltpu.roll(x, LANES - 32, 1)
    return x * cos + jnp.where(first_half, bwd, fwd) * sin


def _in_proj_kernel(h_ref, n1_ref, w_ref, cos_ref, sin_ref, z_ref):
    hn = _rms(h_ref[...], n1_ref[...]).astype(BF16)
    cos = cos_ref[...]
    sin = sin_ref[...]
    lane = lax.broadcasted_iota(I32, (hn.shape[0], LANES), 1)
    first_half = (lane & 32) == 0
    for s in range(IN_WIDTH // SECTION):
        zs = _dot(hn, w_ref[:, s * SECTION:(s + 1) * SECTION])
        if s in (1, 2):
            scale = 1.0 if s == 1 else RET_HEAD_DIM ** -0.5
            for c in range(SECTION // LANES):
                slab = _rope_slab(zs[:, c * LANES:(c + 1) * LANES], cos, sin, first_half)
                z_ref[:, s * SECTION + c * LANES:s * SECTION + (c + 1) * LANES] = slab * scale
        else:
            z_ref[:, s * SECTION:(s + 1) * SECTION] = zs


def _in_proj(h, n1, w_in, i, cos_t, sin_t, tm):
    T = h.shape[0]
    n_tab = cos_t.shape[0] // tm if cos_t.shape[0] > 1 else 1
    tb = tm if cos_t.shape[0] > 1 else 1
    return pl.pallas_call(
        _in_proj_kernel,
        grid=(T // tm,),
        in_specs=[
            pl.BlockSpec((tm, D_MODEL), lambda t: (t, 0)),
            pl.BlockSpec((None, 1, D_MODEL), lambda t: (i, 0, 0)),
            pl.BlockSpec((None, D_MODEL, IN_WIDTH), lambda t: (i, 0, 0)),
            pl.BlockSpec((tb, LANES), lambda t: (t % n_tab, 0)),
            pl.BlockSpec((tb, LANES), lambda t: (t % n_tab, 0)),
        ],
        out_specs=pl.BlockSpec((tm, IN_WIDTH), lambda t: (t, 0)),
        out_shape=jax.ShapeDtypeStruct((T, IN_WIDTH), F32),
        compiler_params=_params("arbitrary"),
        name="in_proj",
    )(h, n1, w_in, cos_t, sin_t)


def _top2(lg, idx, axis, n_idx):
    neg = jnp.float32(-jnp.inf)
    m1 = jnp.max(lg, axis=axis, keepdims=True)
    i1 = jnp.min(jnp.where(lg == m1, idx, n_idx), axis=axis, keepdims=True)
    lg2 = jnp.where(idx == i1, neg, lg)
    m2 = jnp.max(lg2, axis=axis, keepdims=True)
    i2 = jnp.min(jnp.where(lg2 == m2, idx, n_idx), axis=axis, keepdims=True)
    e2 = jnp.exp(m2 - m1)
    den = 1.0 + e2
    return i1, i2, 1.0 / den, e2 / den


def _route(x, rhi_ref, rlo_ref):
    x_hi, x_lo = _split_bf16(x)
    logits = _dot(x_hi, rhi_ref[...]) + _dot(x_lo, rhi_ref[...]) + _dot(x_hi, rlo_ref[...])
    lane = lax.broadcasted_iota(I32, logits.shape, 1)
    lg = jnp.where(lane < N_EXPERTS, logits, jnp.float32(-jnp.inf))
    i1, i2, w1, w2 = _top2(lg, lane, 1, LANES)
    return jnp.where(lane == i1, w1, 0.0) + jnp.where(lane == i2, w2, 0.0)


def _route_logits_t(x, rthi_ref, rtlo_ref):
    x_hi, x_lo = _split_bf16(x)
    return _dot_nt(rthi_ref[...], x_hi) + _dot_nt(rthi_ref[...], x_lo) + _dot_nt(rtlo_ref[...], x_hi)


def _route_ranks(lt, fresh, live, carry_ref, meta_ref, cnt_ref):
    n = lt.shape[1]
    sub = lax.broadcasted_iota(I32, lt.shape, 0)
    i1, i2, w1, w2 = _top2(lt, sub, 0, N_EXPERTS)
    ind = jnp.where(sub == i1, 1.0, 0.0) + jnp.where(sub == i2, 1.0, 0.0)
    tri = jnp.where(lax.broadcasted_iota(I32, (n, n), 0) <= lax.broadcasted_iota(I32, (n, n), 1), 1.0, 0.0)
    csum = _dot(ind.astype(BF16), tri.astype(BF16))
    carry = carry_ref[...] * (1.0 - fresh)
    rank = carry[:, 0:1] + csum - ind
    r1 = jnp.sum(jnp.where(sub == i1, rank, 0.0), axis=0, keepdims=True).astype(I32)
    r2 = jnp.sum(jnp.where(sub == i2, rank, 0.0), axis=0, keepdims=True).astype(I32)
    carry_ref[...] = carry + csum[:, n - 1:n] * live
    cnt_ref[0] = carry_ref[...].astype(I32)
    rows = [i1, i2, r1, r2, pltpu.bitcast(w1, I32), pltpu.bitcast(w2, I32)]
    meta = jnp.zeros(lt.shape, I32)
    for j, rowv in enumerate(rows):
        meta = jnp.where(sub == j, jnp.broadcast_to(rowv, lt.shape), meta)
    meta_ref[0] = meta


def _pool_project(d_slabs, u, wpool_ref, pscale_ref):
    ys = []
    for g in range(len(POOL_WINDOWS)):
        sl = slice(g * POOL_GROUP_DIM, (g + 1) * POOL_GROUP_DIM)
        ys.append(_dot((d_slabs[g] - u[:, sl]).astype(BF16), wpool_ref[g]) * pscale_ref[:, sl])
    return ys


def _mixer_prompt_kernel(*refs, cb, nc, n_chunks, decay_c, moe):
    (u_ref, q_ref, k_ref, v_ref, g_ref, h_ref, dmask_ref, xi_ref, zeta_ref,
     wpool_ref, pscale_ref, wo_ref, n2_ref) = refs[:13]
    if moe:
        (rthi_ref, rtlo_ref, h1_ref, xt_ref, meta_ref, cnt_ref, pool_out_ref, ret_out_ref,
         s_scr, p_scr, mix_scr, carry_scr) = refs[13:]
    else:
        h1_ref, hn2_ref, pool_out_ref, ret_out_ref, s_scr, p_scr, mix_scr = refs[13:]
    step = pl.program_id(0)
    c = jnp.minimum(step, n_chunks - 1) % nc
    c_tail = jnp.maximum(step - 1, 0) % nc

    @pl.when(step == 0)
    def _():
        mix_scr[...] = jnp.zeros_like(mix_scr)
        if moe:
            carry_scr[...] = jnp.zeros_like(carry_scr)

    @pl.when(c == 0)
    def _():
        s_scr[...] = jnp.zeros_like(s_scr)
        p_scr[0:TAIL_ROWS, :] = jnp.zeros((TAIL_ROWS, POOL_WIDTH), F32)

    proj = _dot(mix_scr[...], wo_ref[...])

    u = u_ref[...]
    old_tail = p_scr[0:TAIL_ROWS, :]
    p_scr[TAIL_ROWS:TAIL_ROWS + cb, :] = u
    pos = c * cb + lax.broadcasted_iota(I32, (cb, POOL_GROUP_DIM), 0)
    means = []
    for g, w in enumerate(POOL_WINDOWS):
        sl = slice(g * POOL_GROUP_DIM, (g + 1) * POOL_GROUP_DIM)
        s = u[:, sl]
        for j in range(1, w):
            s = s + p_scr[TAIL_ROWS - j:TAIL_ROWS - j + cb, sl]
        means.append(s / jnp.minimum(pos + 1, w).astype(F32))
    ys = _pool_project(means, u, wpool_ref, pscale_ref)
    new_tail = p_scr[cb:cb + TAIL_ROWS, :]

    h1 = h_ref[...] + proj
    h1_ref[...] = h1
    hn2 = _rms(h1, n2_ref[...])

    q = q_ref[...]
    k = k_ref[...]
    qb = q.astype(BF16)
    kb = k.astype(BF16)
    kz = (k * zeta_ref[...]).astype(BF16)
    vb = v_ref[...].astype(BF16)
    xi = xi_ref[...]
    heads = range(RET_HEADS)
    sls = [slice(hd * RET_HEAD_DIM, (hd + 1) * RET_HEAD_DIM) for hd in heads]
    s_prev = [s_scr[hd] for hd in heads]
    sc = [_dot_nt(qb[:, sl], kb[:, sl]) for sl in sls]
    cross = [_dot(qb[:, sl], s_prev[hd].astype(BF16)) for hd, sl in enumerate(sls)]
    upd = [_dot_tn(kz[:, sl], vb[:, sl]) for sl in sls]

    if moe:
        _to_token_layout(xt_ref, hn2)
        lt = _route_logits_t(hn2, rthi_ref, rtlo_ref)
    else:
        hn2_ref[...] = hn2.astype(BF16)

    pm = [(sc[hd] * dmask_ref[hd]).astype(BF16) for hd in heads]
    o_h = [_dot(pm[hd], vb[:, sl]) + cross[hd] * xi[:, sl] for hd, sl in enumerate(sls)]
    new_s = [s_prev[hd] * decay_c[hd] + upd[hd] for hd in heads]

    if moe:
        _route_ranks(lt, jnp.where(c_tail == 0, 1.0, 0.0), jnp.where(step > 0, 1.0, 0.0), carry_scr, meta_ref, cnt_ref)

    mu = [jnp.mean(o, axis=-1, keepdims=True) for o in o_h]
    cen = [o - m for o, m in zip(o_h, mu)]
    var = [jnp.mean(jnp.square(x), axis=-1, keepdims=True) for x in cen]
    os_ = [x * lax.rsqrt(v + NORM_EPS) for x, v in zip(cen, var)]
    o = jnp.concatenate(os_, axis=1) * jax.nn.silu(g_ref[...])
    mix_scr[...] = jnp.concatenate(ys + [o], axis=1).astype(BF16)

    real = step < n_chunks
    new_tail = jnp.where(real, new_tail, old_tail)
    p_scr[0:TAIL_ROWS, :] = new_tail
    pool_out_ref[0] = new_tail[TAIL_ROWS - POOL_BUF:, :]
    for hd in heads:
        kept = jnp.where(real, new_s[hd], s_prev[hd])
        s_scr[hd] = kept
        ret_out_ref[0, hd] = kept


def _mixer_prompt(z, h, tabs, W, i, B, L, cb):
    T = B * L
    nc = L // cb
    moe = i % 2 == 1
    n_chunks = B * nc
    head = lambda g: jnp.minimum(g, n_chunks - 1)
    tail = lambda g: jnp.maximum(g - 1, 0)
    zspec = lambda s: pl.BlockSpec((cb, SECTION), lambda g, s=s: (head(g), s))
    const2 = lambda shape: pl.BlockSpec(shape, lambda g: (0, 0))
    const3 = lambda shape: pl.BlockSpec(shape, lambda g: (0, 0, 0))
    layer3 = lambda shape: pl.BlockSpec((None,) + shape, lambda g: (i, 0, 0))
    rows = pl.BlockSpec((cb, D_MODEL), lambda g: (tail(g), 0))
    in_specs = [zspec(0), zspec(1), zspec(2), zspec(3), zspec(4), rows,
                const3((RET_HEADS, cb, cb)), const2((cb, RET_WIDTH)), const2((cb, RET_WIDTH)),
                pl.BlockSpec((None, len(POOL_WINDOWS), POOL_GROUP_DIM, POOL_GROUP_DIM), lambda g: (i, 0, 0, 0)),
                layer3((1, POOL_WIDTH)), layer3((D_MODEL, D_MODEL)), layer3((1, D_MODEL))]
    args = [z, z, z, z, z, h, tabs["dmask"], tabs["xi"], tabs["zeta"],
            W["w_pool"], W["pool_scale"], W["w_o"], W["norm2"]]
    scratch = [pltpu.VMEM((RET_HEADS, RET_HEAD_DIM, RET_HEAD_DIM), F32),
               pltpu.VMEM((TAIL_ROWS + cb, POOL_WIDTH), F32),
               pltpu.VMEM((cb, D_MODEL), BF16)]
    state_specs = [pl.BlockSpec((1, POOL_BUF, POOL_WIDTH), lambda g: (head(g) // nc, 0, 0)),
                   pl.BlockSpec((1, RET_HEADS, RET_HEAD_DIM, RET_HEAD_DIM), lambda g: (head(g) // nc, 0, 0, 0))]
    state_shapes = [jax.ShapeDtypeStruct((B, POOL_BUF, POOL_WIDTH), F32),
                    jax.ShapeDtypeStruct((B, RET_HEADS, RET_HEAD_DIM, RET_HEAD_DIM), F32)]
    if moe:
        j = i // 2
        rt = pl.BlockSpec((None, N_EXPERTS, D_MODEL), lambda g: (j, 0, 0))
        in_specs += [rt, rt]
        args += [W["router_t_hi"], W["router_t_lo"]]
        out_specs = [rows,
                     pl.BlockSpec((cb * TOKEN_TILES, LANES), lambda g: (tail(g), 0)),
                     pl.BlockSpec((1, SUBLANES, cb), lambda g: (tail(g), 0, 0)),
                     pl.BlockSpec((1, N_EXPERTS, LANES), lambda g: (tail(g) // nc, 0, 0))]
        out_shape = [jax.ShapeDtypeStruct((T, D_MODEL), F32),
                     jax.ShapeDtypeStruct((T * TOKEN_TILES, LANES), F32),
                     jax.ShapeDtypeStruct((B * nc, SUBLANES, cb), I32),
                     jax.ShapeDtypeStruct((B, N_EXPERTS, LANES), I32)]
        scratch.append(pltpu.VMEM((N_EXPERTS, LANES), F32))
    else:
        out_specs = [rows, rows]
        out_shape = [jax.ShapeDtypeStruct((T, D_MODEL), F32), jax.ShapeDtypeStruct((T, D_MODEL), BF16)]
    return pl.pallas_call(
        functools.partial(_mixer_prompt_kernel, cb=cb, nc=nc, n_chunks=n_chunks, decay_c=tabs["decay_c"], moe=moe),
        grid=(n_chunks + 1,),
        in_specs=in_specs,
        out_specs=out_specs + state_specs,
        out_shape=out_shape + state_shapes,
        scratch_shapes=scratch,
        compiler_params=_params("arbitrary"),
        name="mixer_prompt",
    )(*args)


def _to_head_rows(ref, x):
    n = x.shape[0]
    for hd in range(RET_HEADS):
        ref[pl.ds(hd, n, stride=RET_HEADS), 0:RET_HEAD_DIM] = x[:, hd * RET_HEAD_DIM:(hd + 1) * RET_HEAD_DIM]


def _mixer_sample_kernel(*refs, bb, decay_1, moe):
    (u_ref, q_ref, k_ref, v_ref, g_ref, h_ref, buf_ref, s_ref,
     wpool_ref, pscale_ref, wo_ref, n2_ref) = refs[:12]
    n_in = 14 if moe else 12
    router_refs = refs[12:n_in]
    h1_ref, hn2_ref = refs[n_in:n_in + 2]
    gates_ref = refs[n_in + 2] if moe else None
    buf_out_ref, s_out_ref, q2_scr, k2_scr, v2_scr, g2_scr, c2_scr, o2_scr, qrep_scr, krep_scr = refs[-10:]
    dh = RET_HEAD_DIM
    grp = SUBLANES

    u = u_ref[...]
    means = []
    for g, w in enumerate(POOL_WINDOWS):
        s = u[:, g * POOL_GROUP_DIM:(g + 1) * POOL_GROUP_DIM]
        for j in range(1, w):
            lo = (POOL_BUF - j) * POOL_WIDTH + g * POOL_GROUP_DIM
            s = s + buf_ref[:, lo:lo + POOL_GROUP_DIM]
        means.append(s / float(min(PAST_LEN + 1, w)))
    ys = _pool_project(means, u, wpool_ref, pscale_ref)
    keep = (POOL_BUF - 1) * POOL_WIDTH
    buf_out_ref[:, 0:keep] = buf_ref[:, POOL_WIDTH:POOL_WIDTH + keep]
    buf_out_ref[:, keep:keep + POOL_WIDTH] = u

    for ref, src in ((q2_scr, q_ref), (k2_scr, k_ref), (v2_scr, v_ref), (g2_scr, g_ref)):
        _to_head_rows(ref, src[...])
    spread = jnp.where(lax.broadcasted_iota(I32, (grp, grp * LANES), 0)
                       == lax.broadcasted_iota(I32, (grp, grp * LANES), 1) // LANES, 1.0, 0.0)

    def per_group(gi, carry):
        rows = pl.ds(pl.multiple_of(gi * grp, grp), grp)
        qrep_scr[...] = _dot_tn(q_ref[rows, :], spread)
        krep_scr[...] = _dot_tn(k_ref[rows, :], spread)
        for bl in range(grp):
            b = gi * grp + bl
            cross = []
            for hd in range(RET_HEADS):
                tile = (slice(hd * dh, (hd + 1) * dh), slice(bl * LANES, bl * LANES + dh))
                s_prev = s_ref[b, hd]
                cross.append(jnp.sum(s_prev * qrep_scr[tile], axis=0, keepdims=True))
                v_row = v2_scr[pl.ds(b * RET_HEADS + hd, 1), 0:dh]
                s_out_ref[b, hd] = s_prev * decay_1[hd] + krep_scr[tile] * v_row
            c2_scr[pl.ds(pl.multiple_of(b * RET_HEADS, RET_HEADS), RET_HEADS), 0:dh] = jnp.concatenate(cross, axis=0)
        return carry

    lax.fori_loop(0, bb // grp, per_group, 0)

    n2 = bb * RET_HEADS
    q2, k2, v2 = q2_scr[:, 0:dh], k2_scr[:, 0:dh], v2_scr[:, 0:dh]
    head = lax.broadcasted_iota(I32, (n2, dh), 0) % RET_HEADS
    gamma = jnp.zeros((n2, dh), F32)
    for hd in range(RET_HEADS):
        gamma = jnp.where(head == hd, decay_1[hd], gamma)
    o2 = jnp.sum(q2 * k2, axis=-1, keepdims=True) * v2 + gamma * c2_scr[:, 0:dh]
    mu = jnp.mean(o2, axis=-1, keepdims=True)
    var = jnp.mean(jnp.square(o2 - mu), axis=-1, keepdims=True)
    o2_scr[:, 0:dh] = (o2 - mu) * lax.rsqrt(var + NORM_EPS) * jax.nn.silu(g2_scr[:, 0:dh])
    o = [o2_scr[pl.ds(hd, bb, stride=RET_HEADS), 0:dh] for hd in range(RET_HEADS)]
    mix = jnp.concatenate(ys + o, axis=1)
    h1 = h_ref[...] + _dot(mix.astype(BF16), wo_ref[...])
    h1_ref[...] = h1
    hn2 = _rms(h1, n2_ref[...])
    hn2_ref[...] = hn2.astype(BF16)
    if moe:
        gates_ref[...] = _route(hn2, *router_refs)


def _mixer_sample(z, h, bufs, states, decay_1, W, i, bb):
    B = h.shape[0]
    moe = i % 2 == 1
    flat = POOL_BUF * POOL_WIDTH
    zspec = lambda s: pl.BlockSpec((bb, SECTION), lambda t, s=s: (t, s))
    layer3 = lambda shape: pl.BlockSpec((None,) + shape, lambda t: (i, 0, 0))
    rows = pl.BlockSpec((bb, D_MODEL), lambda t: (t, 0))
    state_shape = (bb, RET_HEADS, RET_HEAD_DIM, RET_HEAD_DIM)
    in_specs = [zspec(0), zspec(1), zspec(2), zspec(3), zspec(4), rows,
                pl.BlockSpec((None, bb, flat), lambda t: (i, t, 0)),
                pl.BlockSpec((None,) + state_shape, lambda t: (i, t, 0, 0, 0)),
                pl.BlockSpec((None, len(POOL_WINDOWS), POOL_GROUP_DIM, POOL_GROUP_DIM), lambda t: (i, 0, 0, 0)),
                layer3((1, POOL_WIDTH)), layer3((D_MODEL, D_MODEL)), layer3((1, D_MODEL))]
    args = [z, z, z, z, z, h, bufs, states, W["w_pool"], W["pool_scale"], W["w_o"], W["norm2"]]
    out_specs = [rows, rows]
    out_shape = [jax.ShapeDtypeStruct((B, D_MODEL), F32), jax.ShapeDtypeStruct((B, D_MODEL), BF16)]
    if moe:
        j = i // 2
        r = pl.BlockSpec((None, D_MODEL, LANES), lambda t: (j, 0, 0))
        in_specs += [r, r]
        args += [W["router_hi"], W["router_lo"]]
        out_specs.append(pl.BlockSpec((bb, LANES), lambda t: (t, 0)))
        out_shape.append(jax.ShapeDtypeStruct((B, LANES), F32))
    out_specs += [pl.BlockSpec((bb, flat), lambda t: (t, 0)), pl.BlockSpec(state_shape, lambda t: (t, 0, 0, 0))]
    out_shape += [jax.ShapeDtypeStruct((B, flat), F32), jax.ShapeDtypeStruct(states.shape[1:], F32)]
    outs = pl.pallas_call(
        functools.partial(_mixer_sample_kernel, bb=bb, decay_1=decay_1, moe=moe),
        grid=(B // bb,),
        in_specs=in_specs,
        out_specs=out_specs,
        out_shape=out_shape,
        scratch_shapes=[pltpu.VMEM((bb * RET_HEADS, LANES), F32) for _ in range(6)]
        + [pltpu.VMEM((RET_WIDTH, SUBLANES * LANES), F32) for _ in range(2)],
        compiler_params=_params("arbitrary"),
        name="mixer_sample",
    )(*args)
    if moe:
        h1, hn2, gates, pool, ret = outs
    else:
        (h1, hn2, pool, ret), gates = outs, None
    return h1, hn2, gates, pool, ret


def _swiglu(x, wg_ref, wu_ref, wd_ref, between_chunks=None):
    y = None
    for c, (lo, hi) in enumerate(MOE_FF_CHUNKS):
        if between_chunks is not None:
            between_chunks(c)
        hidden = jax.nn.silu(_dot(x, wg_ref[:, lo:hi])) * _dot(x, wu_ref[:, lo:hi])
        part = _dot(hidden.astype(BF16), wd_ref[lo:hi, :])
        y = part if y is None else y + part
    return y


def _ffn_kernel(*refs, moe):
    if moe:
        x_ref, h_ref, gates_ref, wg_ref, wu_ref, wd_ref, out_ref, acc_ref = refs
    else:
        x_ref, h_ref, wg_ref, wu_ref, wd_ref, out_ref, acc_ref = refs
    f = pl.program_id(1)
    y = _swiglu(x_ref[...], wg_ref, wu_ref, wd_ref)
    if moe:
        gates = gates_ref[...]
        lane = lax.broadcasted_iota(I32, gates.shape, 1)
        y = y * jnp.sum(jnp.where(lane == f, gates, 0.0), axis=-1, keepdims=True)

    @pl.when(f == 0)
    def _():
        acc_ref[...] = y

    @pl.when(f > 0)
    def _():
        acc_ref[...] += y

    @pl.when(f == pl.num_programs(1) - 1)
    def _():
        out_ref[...] = h_ref[...] + acc_ref[...]


def _ffn(x, h, gates, wg, wu, wd, j, tm):
    T = x.shape[0]
    moe = gates is not None
    tf = D_FF_TILE
    row = pl.BlockSpec((tm, D_MODEL), lambda t, f: (t, 0))
    in_specs = [row, row]
    args = [x, h]
    if moe:
        nf = wg.shape[1]
        in_specs.append(pl.BlockSpec((tm, LANES), lambda t, f: (t, 0)))
        args.append(gates)
        in_specs += [pl.BlockSpec((None, None, D_MODEL, tf), lambda t, f: (j, f, 0, 0)),
                     pl.BlockSpec((None, None, D_MODEL, tf), lambda t, f: (j, f, 0, 0)),
                     pl.BlockSpec((None, None, tf, D_MODEL), lambda t, f: (j, f, 0, 0))]
    else:
        nf = wg.shape[2] // tf
        in_specs += [pl.BlockSpec((None, D_MODEL, tf), lambda t, f: (j, 0, f)),
                     pl.BlockSpec((None, D_MODEL, tf), lambda t, f: (j, 0, f)),
                     pl.BlockSpec((None, tf, D_MODEL), lambda t, f: (j, f, 0))]
    args += [wg, wu, wd]
    return pl.pallas_call(
        functools.partial(_ffn_kernel, moe=moe),
        grid=(T // tm, nf),
        in_specs=in_specs,
        out_specs=row,
        out_shape=jax.ShapeDtypeStruct((T, D_MODEL), F32),
        scratch_shapes=[pltpu.VMEM((tm, D_MODEL), F32)],
        compiler_params=_params("arbitrary", "arbitrary"),
        name="ffn_moe" if moe else "ffn_dense",
    )(*args)


def _moe_sparse_kernel(dest_ref, w1_ref, w2_ref, tile_e_ref, tile_rows_ref, n_valid_ref,
                       xt_ref, wg_ref, wu_ref, wd_ref, out_ref,
                       src_ref, ws_ref, g_ref, y_ref, *, tb):
    b = pl.program_id(0)
    j = pl.program_id(1)
    n_steps = pl.num_programs(1)
    n_valid = n_valid_ref[b]
    ts = MOE_TILE
    unroll = SUBLANES
    tile8 = lambda r: pl.ds(pl.multiple_of(r * TOKEN_TILES, TOKEN_TILES), TOKEN_TILES)

    rows8 = lambda start: pl.ds(pl.multiple_of(start, TOKEN_TILES), TOKEN_TILES)
    slot_rows = tb * TOKEN_TILES

    def gather_row(tile, buf, r):
        g_ref[buf, tile8(r), :] = xt_ref[rows8(src_ref[tile * ts + r] & (slot_rows - 1)), :]

    def scatter_row(tile, buf, r):
        out_ref[rows8(src_ref[tile * ts + r]), :] = ws_ref[tile * ts + r] * y_ref[buf, tile8(r), :]

    def row_loop(row_fn, tile, buf):
        def group(g, carry):
            for u in range(unroll):
                row_fn(tile, buf, g * unroll + u)
            return carry
        lax.fori_loop(0, ts // unroll, group, 0)

    @pl.when((b == 0) & (j == 0))
    def _():
        g_ref[...] = jnp.zeros_like(g_ref)
        y_ref[...] = jnp.zeros_like(y_ref)

    @pl.when(j == 0)
    def _():
        def fill(t, carry):
            d = dest_ref[b * tb + t]
            d1 = d & 0xFFFF
            d2 = d >> 16
            src_ref[d1] = t * TOKEN_TILES
            src_ref[d2] = t * TOKEN_TILES + slot_rows
            ws_ref[d1] = w1_ref[b * tb + t]
            ws_ref[d2] = w2_ref[b * tb + t]
            return carry

        lax.fori_loop(0, tb, fill, 0, unroll=unroll)

        def pad_tile(tile, carry):
            n_rows = tile_rows_ref[b * n_steps + tile]
            last = tile * ts + n_rows - 1

            def pad_row(r, c):
                src_ref[tile * ts + r] = src_ref[last]
                ws_ref[tile * ts + r] = ws_ref[last]
                return c

            return lax.fori_loop(n_rows, ts, pad_row, carry)

        lax.fori_loop(0, n_valid, pad_tile, 0)
        row_loop(gather_row, 0, 0)

    @pl.when(j < n_valid)
    def _():
        cur = j & 1
        nxt = 1 - cur
        tile_next = jnp.minimum(j + 1, n_valid - 1)
        tile_prev = jnp.maximum(j - 1, 0)
        x = _from_token_layout(g_ref.at[cur], ts).astype(BF16)
        n_chunks = len(MOE_FF_CHUNKS)

        def move_rows(c):
            for r in range(c * ts // n_chunks, (c + 1) * ts // n_chunks):
                gather_row(tile_next, nxt, r)
                scatter_row(tile_prev, nxt, r)

        _to_token_layout(y_ref.at[cur], _swiglu(x, wg_ref, wu_ref, wd_ref, move_rows))

    @pl.when(j == n_valid)
    def _():
        row_loop(scatter_row, n_valid - 1, (n_valid - 1) & 1)


def _moe_plan(meta, cnt, B, L, cb):
    ts = MOE_TILE
    n_steps = TOP_K * L // ts + N_EXPERTS + 1
    m = meta.reshape(B, L // cb, SUBLANES, cb).transpose(2, 0, 1, 3).reshape(SUBLANES, B, L)
    e1, e2, r1, r2 = m[0], m[1], m[2], m[3]
    w1 = lax.bitcast_convert_type(m[4], F32).reshape(B * L)
    w2 = lax.bitcast_convert_type(m[5], F32).reshape(B * L)
    cnt = cnt[:, :, 0]
    padded = (cnt + ts - 1) // ts * ts
    ends = jnp.cumsum(padded, axis=1)
    off = ends - padded
    experts = jnp.arange(N_EXPERTS, dtype=I32)

    def pick(table, idx):
        return jnp.sum(jnp.where(idx[:, :, None] == experts, table[:, None, :], 0), axis=2)

    d1 = pick(off, e1) + r1
    d2 = pick(off, e2) + r2
    dest = (d1 | (d2 << 16)).reshape(B * L).astype(I32)
    start = jnp.arange(n_steps, dtype=I32) * ts
    tile_e = jnp.sum((ends[:, None, :] <= start[None, :, None]).astype(I32), axis=2)
    valid = start[None, :] < ends[:, -1:]
    last_e = jnp.max(jnp.where(valid, tile_e, 0), axis=1, keepdims=True)
    tile_e = jnp.where(valid, tile_e, last_e)
    filled = pick(cnt, tile_e) - (start[None, :] - pick(off, tile_e))
    tile_rows = jnp.where(valid, jnp.clip(filled, 0, ts), 0)
    n_valid = (ends[:, -1] // ts).astype(I32)
    return dest, w1, w2, tile_e.reshape(-1).astype(I32), tile_rows.reshape(-1).astype(I32), n_valid, n_steps


def _moe_sparse(xt, plan, wg, wu, wd, j, B, L):
    dest, w1, w2, tile_e, tile_rows, n_valid, n_steps = plan
    ts = MOE_TILE
    tf = D_FF_TILE
    slot_rows = L * TOKEN_TILES
    assert slot_rows & (slot_rows - 1) == 0, "the kernel masks row offsets with slot_rows - 1"
    tok = pl.BlockSpec((slot_rows, LANES), lambda b, k, *_: (b, 0), pipeline_mode=pl.Buffered(1))
    tok_out = pl.BlockSpec((TOP_K * slot_rows, LANES), lambda b, k, *_: (b, 0), pipeline_mode=pl.Buffered(1))
    expert = lambda b, k, dest, w1, w2, te, tr, nv: (j, te[b * n_steps + k], 0, 0)
    grid_spec = pltpu.PrefetchScalarGridSpec(
        num_scalar_prefetch=6,
        grid=(B, n_steps),
        in_specs=[tok,
                  pl.BlockSpec((None, None, D_MODEL, tf), expert),
                  pl.BlockSpec((None, None, D_MODEL, tf), expert),
                  pl.BlockSpec((None, None, tf, D_MODEL), expert)],
        out_specs=tok_out,
        scratch_shapes=[pltpu.SMEM((n_steps * ts,), I32), pltpu.SMEM((n_steps * ts,), F32),
                        pltpu.VMEM((2, ts * TOKEN_TILES, LANES), F32), pltpu.VMEM((2, ts * TOKEN_TILES, LANES), F32)],
    )
    out = pl.pallas_call(
        functools.partial(_moe_sparse_kernel, tb=L),
        grid_spec=grid_spec,
        out_shape=jax.ShapeDtypeStruct((B * TOP_K * slot_rows, LANES), F32),
        compiler_params=_params("arbitrary", "arbitrary"),
        name="moe_sparse",
    )(dest, w1, w2, tile_e, tile_rows, n_valid, xt, wg, wu, wd)
    return out.reshape(B, TOP_K, slot_rows, LANES)


def _ple_kernel(*refs, final, add_tokens):
    refs = list(refs)
    h_ref = refs.pop(0)
    m_ref = refs.pop(0) if add_tokens else None
    p_ref, wg_ref, wp_ref = refs[:3]
    fn_ref = refs[3] if final else None
    out_ref = refs[-1]
    h = h_ref[...]
    if add_tokens:
        for s in range(TOP_K):
            h = h + _from_token_layout(m_ref.at[s], h.shape[0])
    gate = jax.nn.sigmoid(_dot(h.astype(BF16), wg_ref[...]))
    e = _dot(p_ref[...].astype(BF16), wp_ref[...])
    h3 = h + gate * e
    out_ref[...] = _rms(h3, fn_ref[...]) if final else h3


def _ple(h, m_t, p, W, i, tm):
    T = h.shape[0]
    final = i == DEPTH - 1
    add_tokens = m_t is not None
    rows = pl.BlockSpec((tm, D_MODEL), lambda t: (t, 0))
    in_specs = [rows]
    args = [h]
    if add_tokens:
        per_seq = m_t.shape[2] // (tm * TOKEN_TILES)
        in_specs.append(pl.BlockSpec((None, TOP_K, tm * TOKEN_TILES, LANES),
                                     lambda t: (t // per_seq, 0, t % per_seq, 0)))
        args.append(m_t)
    in_specs += [pl.BlockSpec((None, tm, PLE_DIM), lambda t: (i, t, 0)),
                 pl.BlockSpec((None, D_MODEL, D_MODEL), lambda t: (i, 0, 0)),
                 pl.BlockSpec((None, PLE_DIM, D_MODEL), lambda t: (i, 0, 0))]
    args += [p, W["w_ple_gate"], W["w_ple"]]
    if final:
        in_specs.append(pl.BlockSpec((1, D_MODEL), lambda t: (0, 0)))
        args.append(W["final_norm"])
    return pl.pallas_call(
        functools.partial(_ple_kernel, final=final, add_tokens=add_tokens),
        grid=(T // tm,),
        in_specs=in_specs,
        out_specs=rows,
        out_shape=jax.ShapeDtypeStruct((T, D_MODEL), F32),
        compiler_params=_params("arbitrary"),
        name="ple_final" if final else "ple",
    )(*args)


def _rope_tables(pos):
    half = RET_HEAD_DIM // 2
    inv = ROPE_THETA ** (-np.arange(half, dtype=np.float64) / half)
    ang = np.asarray(pos, np.float64)[:, None] * inv[None, :]
    cos, sin = np.cos(ang), np.sin(ang)
    reps = LANES // RET_HEAD_DIM
    return (jnp.asarray(np.tile(np.concatenate([cos, cos], axis=-1), (1, reps)), F32),
            jnp.asarray(np.tile(np.concatenate([-sin, sin], axis=-1), (1, reps)), F32))


def _log_decay():
    return np.log1p(-np.power(2.0, -5.0 - np.arange(RET_HEADS, dtype=np.float64)))


def _retention_tables(C):
    log_g = _log_decay()
    idx = np.arange(C, dtype=np.float64)
    rel = idx[:, None] - idx[None, :]
    dmask = np.where(rel >= 0, np.exp(log_g[:, None, None] * np.maximum(rel, 0.0)), 0.0)
    xi = np.exp(log_g[None, :] * (idx[:, None] + 1.0))
    zeta = np.exp(log_g[None, :] * (C - 1.0 - idx)[:, None])
    spread = lambda t: jnp.asarray(np.repeat(t, RET_HEAD_DIM, axis=1), F32)
    return {"dmask": jnp.asarray(dmask, F32), "xi": spread(xi), "zeta": spread(zeta), "decay_c": _chunk_decay(C)}


def _chunk_decay(C):
    return tuple(math.exp(lg * C) for lg in _log_decay())


def _choose_tile(T, cap):
    tm = min(T, cap)
    while T % tm:
        tm //= 2
    return tm


def _layer_prompt(i, h, p, W, rope, tabs, B, L):
    T = B * L
    tm = _choose_tile(T, 512)
    cb = _choose_tile(L, 256)
    j = i // 2
    z = _in_proj(h, W["norm1"], W["w_in"], i, rope[0], rope[1], _choose_tile(L, tm))
    if i % 2 == 0:
        h1, hn2, pool, ret = _mixer_prompt(z, h, tabs, W, i, B, L, cb)
        h2 = _ffn(hn2, h1, None, W["ffn_g"], W["ffn_u"], W["ffn_d"], j, tm)
        m_t = None
    else:
        h2, xt, meta, cnt, pool, ret = _mixer_prompt(z, h, tabs, W, i, B, L, cb)
        m_t = _moe_sparse(xt, _moe_plan(meta, cnt, B, L, cb), W["moe_g"], W["moe_u"], W["moe_d"], j, B, L)
    return _ple(h2, m_t, p, W, i, tm), pool, ret


def _layer_sample(i, h, p, W, rope, decay_1, bufs, states):
    B = h.shape[0]
    j = i // 2
    z = _in_proj(h, W["norm1"], W["w_in"], i, rope[0], rope[1], B)
    h1, hn2, gates, pool, ret = _mixer_sample(z, h, bufs, states, decay_1, W, i, _choose_tile(B, 16))
    if i % 2 == 0:
        h2 = _ffn(hn2, h1, None, W["ffn_g"], W["ffn_u"], W["ffn_d"], j, B)
    else:
        h2 = _ffn(hn2, h1, gates, W["moe_g"], W["moe_u"], W["moe_d"], j, B)
    return _ple(h2, None, p, W, i, B), pool, ret


def _prepare_weights(norm1, w_in, w_pool, pool_scale, w_o, norm2, ffn_w_gate, ffn_w_up, ffn_w_down,
                     moe_router, moe_w_gate, moe_w_up, moe_w_down, w_ple, w_ple_gate, final_norm):
    r_hi, r_lo = _split_bf16(jnp.pad(moe_router, ((0, 0), (0, 0), (0, LANES - N_EXPERTS))))
    rt_hi, rt_lo = _split_bf16(moe_router.transpose(0, 2, 1))
    return {
        "norm1": norm1[:, None, :], "norm2": norm2[:, None, :], "final_norm": final_norm[None, :],
        "pool_scale": pool_scale[:, None, :],
        "w_in": w_in.astype(BF16), "w_pool": w_pool.astype(BF16), "w_o": w_o.astype(BF16),
        "ffn_g": ffn_w_gate.astype(BF16), "ffn_u": ffn_w_up.astype(BF16), "ffn_d": ffn_w_down.astype(BF16),
        "moe_g": moe_w_gate.astype(BF16), "moe_u": moe_w_up.astype(BF16), "moe_d": moe_w_down.astype(BF16),
        "router_hi": r_hi, "router_lo": r_lo, "router_t_hi": rt_hi, "router_t_lo": rt_lo,
        "w_ple": w_ple.astype(BF16), "w_ple_gate": w_ple_gate.astype(BF16),
    }


def _trunk(x_prompt, x_sample, state_pool, state_ret, p_prompt, p_sample, W):
    B, L, _ = x_prompt.shape
    Bs = x_sample.shape[0]
    hp = x_prompt.reshape(B * L, D_MODEL)
    hs = x_sample.reshape(Bs, D_MODEL)
    pp = p_prompt.reshape(DEPTH, B * L, PLE_DIM)
    ps = p_sample.reshape(DEPTH, Bs, PLE_DIM)
    rope_p = _rope_tables(np.arange(L))
    rope_s = _rope_tables(PAST_LEN + np.arange(1))
    tabs = _retention_tables(_choose_tile(L, 256))
    decay_1 = _chunk_decay(1)
    pool_p, ret_p, pool_s, ret_s = [], [], [], []
    bufs_s = state_pool.reshape(DEPTH, Bs, POOL_BUF * POOL_WIDTH)
    for i in range(DEPTH):
        hp, bp, sp = _layer_prompt(i, hp, pp, W, rope_p, tabs, B, L)
        hs, bs, ss = _layer_sample(i, hs, ps, W, rope_s, decay_1, bufs_s, state_ret)
        pool_p.append(bp)
        ret_p.append(sp)
        pool_s.append(bs)
        ret_s.append(ss)
    return (hp.reshape(B, L, D_MODEL), hs.reshape(Bs, 1, D_MODEL), jnp.stack(pool_p), jnp.stack(ret_p),
            jnp.stack(pool_s).reshape(DEPTH, Bs, POOL_BUF, POOL_WIDTH), jnp.stack(ret_s))


def kernel(x_prompt, x_sample, state_pool, state_ret, p_prompt, p_sample, norm1, w_in, w_pool, pool_scale, w_o, norm2, ffn_w_gate, ffn_w_up, ffn_w_down, moe_router, moe_w_gate, moe_w_up, moe_w_down, w_ple, w_ple_gate, final_norm):
    W = _prepare_weights(norm1, w_in, w_pool, pool_scale, w_o, norm2, ffn_w_gate, ffn_w_up, ffn_w_down,
                         moe_router, moe_w_gate, moe_w_up, moe_w_down, w_ple, w_ple_gate, final_norm)
    return _trunk(x_prompt, x_sample, state_pool, state_ret, p_prompt, p_sample, W)
```

```python
import functools
import math

import numpy as np

import jax
import jax.numpy as jnp
from jax import lax
from jax.experimental import pallas as pl
from jax.experimental.pallas import tpu as pltpu

F32 = jnp.float32
BF16 = jnp.bfloat16
I32 = jnp.int32

D_MODEL = 1024
DEPTH = 4
PAST_LEN = 16384
POOL_WIDTH = 512
POOL_WINDOWS = (2, 4, 8, 16)
POOL_GROUP_DIM = 128
POOL_BUF = 15
RET_HEADS = 8
RET_HEAD_DIM = 64
RET_WIDTH = 512
ROPE_THETA = 10000.0
IN_WIDTH = POOL_WIDTH + 4 * RET_WIDTH
SECTION = 512
D_FF_TILE = 1408
N_EXPERTS = 8
TOP_K = 2
PLE_DIM = 256
NORM_EPS = 1e-6
LANES = 128
SUBLANES = 8
TOKEN_TILES = D_MODEL // LANES
assert TOKEN_TILES == SUBLANES
TAIL_ROWS = 16
MOE_TILE = 256
MOE_FF_CHUNKS = ((0, 512), (512, 1024), (1024, D_FF_TILE))
VMEM_LIMIT = 58 * 1024 * 1024


def _rms(x, g):
    return x * lax.rsqrt(jnp.mean(x * x, axis=-1, keepdims=True) + NORM_EPS) * g


def _dot(a, b):
    return jnp.dot(a, b, preferred_element_type=F32)


def _dot_nt(a, b):
    return lax.dot_general(a, b, (((1,), (1,)), ((), ())), preferred_element_type=F32)


def _dot_tn(a, b):
    return lax.dot_general(a, b, (((0,), (0,)), ((), ())), preferred_element_type=F32)


def _params(*sem):
    return pltpu.CompilerParams(dimension_semantics=sem, vmem_limit_bytes=VMEM_LIMIT)


def _split_bf16(x):
    hi = x.astype(BF16)
    return hi, (x - hi.astype(F32)).astype(BF16)


def _to_token_layout(ref, x):
    n = x.shape[0]
    for j in range(TOKEN_TILES):
        ref[pl.ds(j, n, stride=TOKEN_TILES), :] = x[:, j * LANES:(j + 1) * LANES]


def _from_token_layout(ref, n):
    return jnp.concatenate([ref[pl.ds(j, n, stride=TOKEN_TILES), :] for j in range(TOKEN_TILES)], axis=1)


def _rope_slab(x, cos, sin, first_half):
    fwd = pltpu.roll(x, 32, 1)
    bwd = pltpu.roll(x, LANES - 32, 1)
    return x * cos + jnp.where(first_half, bwd, fwd) * sin


def _in_proj_kernel(h_ref, n1_ref, w_ref, cos_ref, sin_ref, z_ref):
    hn = _rms(h_ref[...], n1_ref[...]).astype(BF16)
    cos = cos_ref[...]
    sin = sin_ref[...]
    lane = lax.broadcasted_iota(I32, (hn.shape[0], LANES), 1)
    first_half = (lane & 32) == 0
    for s in range(IN_WIDTH // SECTION):
        zs = _dot(hn, w_ref[:, s * SECTION:(s + 1) * SECTION])
        if s in (1, 2):
            scale = 1.0 if s == 1 else RET_HEAD_DIM ** -0.5
            for c in range(SECTION // LANES):
                slab = _rope_slab(zs[:, c * LANES:(c + 1) * LANES], cos, sin, first_half)
                z_ref[:, s * SECTION + c * LANES:s * SECTION + (c + 1) * LANES] = slab * scale
        else:
            z_ref[:, s * SECTION:(s + 1) * SECTION] = zs


def _in_proj(h, n1, w_in, i, cos_t, sin_t, tm):
    T = h.shape[0]
    n_tab = cos_t.shape[0] // tm if cos_t.shape[0] > 1 else 1
    tb = tm if cos_t.shape[0] > 1 else 1
    return pl.pallas_call(
        _in_proj_kernel,
        grid=(T // tm,),
        in_specs=[
            pl.BlockSpec((tm, D_MODEL), lambda t: (t, 0)),
            pl.BlockSpec((None, 1, D_MODEL), lambda t: (i, 0, 0)),
            pl.BlockSpec((None, D_MODEL, IN_WIDTH), lambda t: (i, 0, 0)),
            pl.BlockSpec((tb, LANES), lambda t: (t % n_tab, 0)),
            pl.BlockSpec((tb, LANES), lambda t: (t % n_tab, 0)),
        ],
        out_specs=pl.BlockSpec((tm, IN_WIDTH), lambda t: (t, 0)),
        out_shape=jax.ShapeDtypeStruct((T, IN_WIDTH), F32),
        compiler_params=_params("arbitrary"),
        name="in_proj",
    )(h, n1, w_in, cos_t, sin_t)


def _top2(lg, idx, axis, n_idx):
    neg = jnp.float32(-jnp.inf)
    m1 = jnp.max(lg, axis=axis, keepdims=True)
    i1 = jnp.min(jnp.where(lg == m1, idx, n_idx), axis=axis, keepdims=True)
    lg2 = jnp.where(idx == i1, neg, lg)
    m2 = jnp.max(lg2, axis=axis, keepdims=True)
    i2 = jnp.min(jnp.where(lg2 == m2, idx, n_idx), axis=axis, keepdims=True)
    e2 = jnp.exp(m2 - m1)
    den = 1.0 + e2
    return i1, i2, 1.0 / den, e2 / den


def _route(x, rhi_ref, rlo_ref):
    x_hi, x_lo = _split_bf16(x)
    logits = _dot(x_hi, rhi_ref[...]) + _dot(x_lo, rhi_ref[...]) + _dot(x_hi, rlo_ref[...])
    lane = lax.broadcasted_iota(I32, logits.shape, 1)
    lg = jnp.where(lane < N_EXPERTS, logits, jnp.float32(-jnp.inf))
    i1, i2, w1, w2 = _top2(lg, lane, 1, LANES)
    return jnp.where(lane == i1, w1, 0.0) + jnp.where(lane == i2, w2, 0.0)


def _route_logits_t(x, rthi_ref, rtlo_ref):
    x_hi, x_lo = _split_bf16(x)
    return _dot_nt(rthi_ref[...], x_hi) + _dot_nt(rthi_ref[...], x_lo) + _dot_nt(rtlo_ref[...], x_hi)


def _route_ranks(lt, fresh, live, carry_ref, meta_ref, cnt_ref):
    n = lt.shape[1]
    sub = lax.broadcasted_iota(I32, lt.shape, 0)
    i1, i2, w1, w2 = _top2(lt, sub, 0, N_EXPERTS)
    ind = jnp.where(sub == i1, 1.0, 0.0) + jnp.where(sub == i2, 1.0, 0.0)
    tri = jnp.where(lax.broadcasted_iota(I32, (n, n), 0) <= lax.broadcasted_iota(I32, (n, n), 1), 1.0, 0.0)
    csum = _dot(ind.astype(BF16), tri.astype(BF16))
    carry = carry_ref[...] * (1.0 - fresh)
    rank = carry[:, 0:1] + csum - ind
    r1 = jnp.sum(jnp.where(sub == i1, rank, 0.0), axis=0, keepdims=True).astype(I32)
    r2 = jnp.sum(jnp.where(sub == i2, rank, 0.0), axis=0, keepdims=True).astype(I32)
    carry_ref[...] = carry + csum[:, n - 1:n] * live
    cnt_ref[0] = carry_ref[...].astype(I32)
    rows = [i1, i2, r1, r2, pltpu.bitcast(w1, I32), pltpu.bitcast(w2, I32)]
    meta = jnp.zeros(lt.shape, I32)
    for j, rowv in enumerate(rows):
        meta = jnp.where(sub == j, jnp.broadcast_to(rowv, lt.shape), meta)
    meta_ref[0] = meta


def _pool_project(d_slabs, u, wpool_ref, pscale_ref):
    ys = []
    for g in range(len(POOL_WINDOWS)):
        sl = slice(g * POOL_GROUP_DIM, (g + 1) * POOL_GROUP_DIM)
        ys.append(_dot((d_slabs[g] - u[:, sl]).astype(BF16), wpool_ref[g]) * pscale_ref[:, sl])
    return ys


def _mixer_prompt_kernel(*refs, cb, nc, n_chunks, decay_c, moe):
    (u_ref, q_ref, k_ref, v_ref, g_ref, h_ref, dmask_ref, xi_ref, zeta_ref,
     wpool_ref, pscale_ref, wo_ref, n2_ref) = refs[:13]
    if moe:
        (rthi_ref, rtlo_ref, h1_ref, xt_ref, meta_ref, cnt_ref, pool_out_ref, ret_out_ref,
         s_scr, p_scr, mix_scr, carry_scr) = refs[13:]
    else:
        h1_ref, hn2_ref, pool_out_ref, ret_out_ref, s_scr, p_scr, mix_scr = refs[13:]
    step = pl.program_id(0)
    c = jnp.minimum(step, n_chunks - 1) % nc
    c_tail = jnp.maximum(step - 1, 0) % nc

    @pl.when(step == 0)
    def _():
        mix_scr[...] = jnp.zeros_like(mix_scr)
        if moe:
            carry_scr[...] = jnp.zeros_like(carry_scr)

    @pl.when(c == 0)
    def _():
        s_scr[...] = jnp.zeros_like(s_scr)
        p_scr[0:TAIL_ROWS, :] = jnp.zeros((TAIL_ROWS, POOL_WIDTH), F32)

    proj = _dot(mix_scr[...], wo_ref[...])

    u = u_ref[...]
    old_tail = p_scr[0:TAIL_ROWS, :]
    p_scr[TAIL_ROWS:TAIL_ROWS + cb, :] = u
    pos = c * cb + lax.broadcasted_iota(I32, (cb, POOL_GROUP_DIM), 0)
    means = []
    for g, w in enumerate(POOL_WINDOWS):
        sl = slice(g * POOL_GROUP_DIM, (g + 1) * POOL_GROUP_DIM)
        s = u[:, sl]
        for j in range(1, w):
            s = s + p_scr[TAIL_ROWS - j:TAIL_ROWS - j + cb, sl]
        means.append(s / jnp.minimum(pos + 1, w).astype(F32))
    ys = _pool_project(means, u, wpool_ref, pscale_ref)
    new_tail = p_scr[cb:cb + TAIL_ROWS, :]

    h1 = h_ref[...] + proj
    h1_ref[...] = h1
    hn2 = _rms(h1, n2_ref[...])

    q = q_ref[...]
    k = k_ref[...]
    qb = q.astype(BF16)
    kb = k.astype(BF16)
    kz = (k * zeta_ref[...]).astype(BF16)
    vb = v_ref[...].astype(BF16)
    xi = xi_ref[...]
    heads = range(RET_HEADS)
    sls = [slice(hd * RET_HEAD_DIM, (hd + 1) * RET_HEAD_DIM) for hd in heads]
    s_prev = [s_scr[hd] for hd in heads]
    sc = [_dot_nt(qb[:, sl], kb[:, sl]) for sl in sls]
    cross = [_dot(qb[:, sl], s_prev[hd].astype(BF16)) for hd, sl in enumerate(sls)]
    upd = [_dot_tn(kz[:, sl], vb[:, sl]) for sl in sls]

    if moe:
        _to_token_layout(xt_ref, hn2)
        lt = _route_logits_t(hn2, rthi_ref, rtlo_ref)
    else:
        hn2_ref[...] = hn2.astype(BF16)

    pm = [(sc[hd] * dmask_ref[hd]).astype(BF16) for hd in heads]
    o_h = [_dot(pm[hd], vb[:, sl]) + cross[hd] * xi[:, sl] for hd, sl in enumerate(sls)]
    new_s = [s_prev[hd] * decay_c[hd] + upd[hd] for hd in heads]

    if moe:
        _route_ranks(lt, jnp.where(c_tail == 0, 1.0, 0.0), jnp.where(step > 0, 1.0, 0.0), carry_scr, meta_ref, cnt_ref)

    mu = [jnp.mean(o, axis=-1, keepdims=True) for o in o_h]
    cen = [o - m for o, m in zip(o_h, mu)]
    var = [jnp.mean(jnp.square(x), axis=-1, keepdims=True) for x in cen]
    os_ = [x * lax.rsqrt(v + NORM_EPS) for x, v in zip(cen, var)]
    o = jnp.concatenate(os_, axis=1) * jax.nn.silu(g_ref[...])
    mix_scr[...] = jnp.concatenate(ys + [o], axis=1).astype(BF16)

    real = step < n_chunks
    new_tail = jnp.where(real, new_tail, old_tail)
    p_scr[0:TAIL_ROWS, :] = new_tail
    pool_out_ref[0] = new_tail[TAIL_ROWS - POOL_BUF:, :]
    for hd in heads:
        kept = jnp.where(real, new_s[hd], s_prev[hd])
        s_scr[hd] = kept
        ret_out_ref[0, hd] = kept


def _mixer_prompt(z, h, tabs, W, i, B, L, cb):
    T = B * L
    nc = L // cb
    moe = i % 2 == 1
    n_chunks = B * nc
    head = lambda g: jnp.minimum(g, n_chunks - 1)
    tail = lambda g: jnp.maximum(g - 1, 0)
    zspec = lambda s: pl.BlockSpec((cb, SECTION), lambda g, s=s: (head(g), s))
    const2 = lambda shape: pl.BlockSpec(shape, lambda g: (0, 0))
    const3 = lambda shape: pl.BlockSpec(shape, lambda g: (0, 0, 0))
    layer3 = lambda shape: pl.BlockSpec((None,) + shape, lambda g: (i, 0, 0))
    rows = pl.BlockSpec((cb, D_MODEL), lambda g: (tail(g), 0))
    in_specs = [zspec(0), zspec(1), zspec(2), zspec(3), zspec(4), rows,
                const3((RET_HEADS, cb, cb)), const2((cb, RET_WIDTH)), const2((cb, RET_WIDTH)),
                pl.BlockSpec((None, len(POOL_WINDOWS), POOL_GROUP_DIM, POOL_GROUP_DIM), lambda g: (i, 0, 0, 0)),
                layer3((1, POOL_WIDTH)), layer3((D_MODEL, D_MODEL)), layer3((1, D_MODEL))]
    args = [z, z, z, z, z, h, tabs["dmask"], tabs["xi"], tabs["zeta"],
            W["w_pool"], W["pool_scale"], W["w_o"], W["norm2"]]
    scratch = [pltpu.VMEM((RET_HEADS, RET_HEAD_DIM, RET_HEAD_DIM), F32),
               pltpu.VMEM((TAIL_ROWS + cb, POOL_WIDTH), F32),
               pltpu.VMEM((cb, D_MODEL), BF16)]
    state_specs = [pl.BlockSpec((1, POOL_BUF, POOL_WIDTH), lambda g: (head(g) // nc, 0, 0)),
                   pl.BlockSpec((1, RET_HEADS, RET_HEAD_DIM, RET_HEAD_DIM), lambda g: (head(g) // nc, 0, 0, 0))]
    state_shapes = [jax.ShapeDtypeStruct((B, POOL_BUF, POOL_WIDTH), F32),
                    jax.ShapeDtypeStruct((B, RET_HEADS, RET_HEAD_DIM, RET_HEAD_DIM), F32)]
    if moe:
        j = i // 2
        rt = pl.BlockSpec((None, N_EXPERTS, D_MODEL), lambda g: (j, 0, 0))
        in_specs += [rt, rt]
        args += [W["router_t_hi"], W["router_t_lo"]]
        out_specs = [rows,
                     pl.BlockSpec((cb * TOKEN_TILES, LANES), lambda g: (tail(g), 0)),
                     pl.BlockSpec((1, SUBLANES, cb), lambda g: (tail(g), 0, 0)),
                     pl.BlockSpec((1, N_EXPERTS, LANES), lambda g: (tail(g) // nc, 0, 0))]
        out_shape = [jax.ShapeDtypeStruct((T, D_MODEL), F32),
                     jax.ShapeDtypeStruct((T * TOKEN_TILES, LANES), F32),
                     jax.ShapeDtypeStruct((B * nc, SUBLANES, cb), I32),
                     jax.ShapeDtypeStruct((B, N_EXPERTS, LANES), I32)]
        scratch.append(pltpu.VMEM((N_EXPERTS, LANES), F32))
    else:
        out_specs = [rows, rows]
        out_shape = [jax.ShapeDtypeStruct((T, D_MODEL), F32), jax.ShapeDtypeStruct((T, D_MODEL), BF16)]
    return pl.pallas_call(
        functools.partial(_mixer_prompt_kernel, cb=cb, nc=nc, n_chunks=n_chunks, decay_c=tabs["decay_c"], moe=moe),
        grid=(n_chunks + 1,),
        in_specs=in_specs,
        out_specs=out_specs + state_specs,
        out_shape=out_shape + state_shapes,
        scratch_shapes=scratch,
        compiler_params=_params("arbitrary"),
        name="mixer_prompt",
    )(*args)


def _to_head_rows(ref, x):
    n = x.shape[0]
    for hd in range(RET_HEADS):
        ref[pl.ds(hd, n, stride=RET_HEADS), 0:RET_HEAD_DIM] = x[:, hd * RET_HEAD_DIM:(hd + 1) * RET_HEAD_DIM]


def _mixer_sample_kernel(*refs, bb, decay_1, moe):
    (u_ref, q_ref, k_ref, v_ref, g_ref, h_ref, buf_ref, s_ref,
     wpool_ref, pscale_ref, wo_ref, n2_ref) = refs[:12]
    n_in = 14 if moe else 12
    router_refs = refs[12:n_in]
    h1_ref, hn2_ref = refs[n_in:n_in + 2]
    gates_ref = refs[n_in + 2] if moe else None
    buf_out_ref, s_out_ref, q2_scr, k2_scr, v2_scr, g2_scr, c2_scr, o2_scr, qrep_scr, krep_scr = refs[-10:]
    dh = RET_HEAD_DIM
    grp = SUBLANES

    u = u_ref[...]
    means = []
    for g, w in enumerate(POOL_WINDOWS):
        s = u[:, g * POOL_GROUP_DIM:(g + 1) * POOL_GROUP_DIM]
        for j in range(1, w):
            lo = (POOL_BUF - j) * POOL_WIDTH + g * POOL_GROUP_DIM
            s = s + buf_ref[:, lo:lo + POOL_GROUP_DIM]
        means.append(s / float(min(PAST_LEN + 1, w)))
    ys = _pool_project(means, u, wpool_ref, pscale_ref)
    keep = (POOL_BUF - 1) * POOL_WIDTH
    buf_out_ref[:, 0:keep] = buf_ref[:, POOL_WIDTH:POOL_WIDTH + keep]
    buf_out_ref[:, keep:keep + POOL_WIDTH] = u

    for ref, src in ((q2_scr, q_ref), (k2_scr, k_ref), (v2_scr, v_ref), (g2_scr, g_ref)):
        _to_head_rows(ref, src[...])
    spread = jnp.where(lax.broadcasted_iota(I32, (grp, grp * LANES), 0)
                       == lax.broadcasted_iota(I32, (grp, grp * LANES), 1) // LANES, 1.0, 0.0)

    def per_group(gi, carry):
        rows = pl.ds(pl.multiple_of(gi * grp, grp), grp)
        qrep_scr[...] = _dot_tn(q_ref[rows, :], spread)
        krep_scr[...] = _dot_tn(k_ref[rows, :], spread)
        for bl in range(grp):
            b = gi * grp + bl
            cross = []
            for hd in range(RET_HEADS):
                tile = (slice(hd * dh, (hd + 1) * dh), slice(bl * LANES, bl * LANES + dh))
                s_prev = s_ref[b, hd]
                cross.append(jnp.sum(s_prev * qrep_scr[tile], axis=0, keepdims=True))
                v_row = v2_scr[pl.ds(b * RET_HEADS + hd, 1), 0:dh]
                s_out_ref[b, hd] = s_prev * decay_1[hd] + krep_scr[tile] * v_row
            c2_scr[pl.ds(pl.multiple_of(b * RET_HEADS, RET_HEADS), RET_HEADS), 0:dh] = jnp.concatenate(cross, axis=0)
        return carry

    lax.fori_loop(0, bb // grp, per_group, 0)

    n2 = bb * RET_HEADS
    q2, k2, v2 = q2_scr[:, 0:dh], k2_scr[:, 0:dh], v2_scr[:, 0:dh]
    head = lax.broadcasted_iota(I32, (n2, dh), 0) % RET_HEADS
    gamma = jnp.zeros((n2, dh), F32)
    for hd in range(RET_HEADS):
        gamma = jnp.where(head == hd, decay_1[hd], gamma)
    o2 = jnp.sum(q2 * k2, axis=-1, keepdims=True) * v2 + gamma * c2_scr[:, 0:dh]
    mu = jnp.mean(o2, axis=-1, keepdims=True)
    var = jnp.mean(jnp.square(o2 - mu), axis=-1, keepdims=True)
    o2_scr[:, 0:dh] = (o2 - mu) * lax.rsqrt(var + NORM_EPS) * jax.nn.silu(g2_scr[:, 0:dh])
    o = [o2_scr[pl.ds(hd, bb, stride=RET_HEADS), 0:dh] for hd in range(RET_HEADS)]
    mix = jnp.concatenate(ys + o, axis=1)
    h1 = h_ref[...] + _dot(mix.astype(BF16), wo_ref[...])
    h1_ref[...] = h1
    hn2 = _rms(h1, n2_ref[...])
    hn2_ref[...] = hn2.astype(BF16)
    if moe:
        gates_ref[...] = _route(hn2, *router_refs)


def _mixer_sample(z, h, buf, s0, decay_1, W, i, bb):
    B = h.shape[0]
    moe = i % 2 == 1
    flat = POOL_BUF * POOL_WIDTH
    zspec = lambda s: pl.BlockSpec((bb, SECTION), lambda t, s=s: (t, s))
    layer3 = lambda shape: pl.BlockSpec((None,) + shape, lambda t: (i, 0, 0))
    rows = pl.BlockSpec((bb, D_MODEL), lambda t: (t, 0))
    pool = pl.BlockSpec((bb, flat), lambda t: (t, 0))
    state = pl.BlockSpec((bb, RET_HEADS, RET_HEAD_DIM, RET_HEAD_DIM), lambda t: (t, 0, 0, 0))
    in_specs = [zspec(0), zspec(1), zspec(2), zspec(3), zspec(4), rows, pool, state,
                pl.BlockSpec((None, len(POOL_WINDOWS), POOL_GROUP_DIM, POOL_GROUP_DIM), lambda t: (i, 0, 0, 0)),
                layer3((1, POOL_WIDTH)), layer3((D_MODEL, D_MODEL)), layer3((1, D_MODEL))]
    args = [z, z, z, z, z, h, buf, s0, W["w_pool"], W["pool_scale"], W["w_o"], W["norm2"]]
    out_specs = [rows, rows]
    out_shape = [jax.ShapeDtypeStruct((B, D_MODEL), F32), jax.ShapeDtypeStruct((B, D_MODEL), BF16)]
    if moe:
        j = i // 2
        r = pl.BlockSpec((None, D_MODEL, LANES), lambda t: (j, 0, 0))
        in_specs += [r, r]
        args += [W["router_hi"], W["router_lo"]]
        out_specs.append(pl.BlockSpec((bb, LANES), lambda t: (t, 0)))
        out_shape.append(jax.ShapeDtypeStruct((B, LANES), F32))
    out_specs += [pool, state]
    out_shape += [jax.ShapeDtypeStruct((B, flat), F32), jax.ShapeDtypeStruct(s0.shape, F32)]
    outs = pl.pallas_call(
        functools.partial(_mixer_sample_kernel, bb=bb, decay_1=decay_1, moe=moe),
        grid=(B // bb,),
        in_specs=in_specs,
        out_specs=out_specs,
        out_shape=out_shape,
        scratch_shapes=[pltpu.VMEM((bb * RET_HEADS, LANES), F32) for _ in range(6)]
        + [pltpu.VMEM((RET_WIDTH, SUBLANES * LANES), F32) for _ in range(2)],
        compiler_params=_params("arbitrary"),
        name="mixer_sample",
    )(*args)
    if moe:
        h1, hn2, gates, pool, ret = outs
    else:
        (h1, hn2, pool, ret), gates = outs, None
    return h1, hn2, gates, pool, ret


def _swiglu(x, wg_ref, wu_ref, wd_ref, between_chunks=None):
    y = None
    for c, (lo, hi) in enumerate(MOE_FF_CHUNKS):
        if between_chunks is not None:
            between_chunks(c)
        hidden = jax.nn.silu(_dot(x, wg_ref[:, lo:hi])) * _dot(x, wu_ref[:, lo:hi])
        part = _dot(hidden.astype(BF16), wd_ref[lo:hi, :])
        y = part if y is None else y + part
    return y


def _ple_update(h, p_ref, wg_ref, wp_ref):
    gate = jax.nn.sigmoid(_dot(h.astype(BF16), wg_ref[...]))
    return h + gate * _dot(p_ref[...].astype(BF16), wp_ref[...])


def _ffn_kernel(*refs, moe, ple):
    refs = list(refs)
    x_ref, h_ref = refs[:2]
    del refs[:2]
    gates_ref = refs.pop(0) if moe else None
    wg_ref, wu_ref, wd_ref = refs[:3]
    ple_refs = refs[3:6] if ple else None
    out_ref, acc_ref = refs[-2:]
    f = pl.program_id(1)
    y = _swiglu(x_ref[...], wg_ref, wu_ref, wd_ref)
    if moe:
        gates = gates_ref[...]
        lane = lax.broadcasted_iota(I32, gates.shape, 1)
        y = y * jnp.sum(jnp.where(lane == f, gates, 0.0), axis=-1, keepdims=True)

    @pl.when(f == 0)
    def _():
        acc_ref[...] = y

    @pl.when(f > 0)
    def _():
        acc_ref[...] += y

    @pl.when(f == pl.num_programs(1) - 1)
    def _():
        h2 = h_ref[...] + acc_ref[...]
        out_ref[...] = _ple_update(h2, *ple_refs) if ple else h2


def _ffn(x, h, gates, wg, wu, wd, j, tm, ple=None):
    T = x.shape[0]
    moe = gates is not None
    tf = D_FF_TILE
    row = pl.BlockSpec((tm, D_MODEL), lambda t, f: (t, 0))
    in_specs = [row, row]
    args = [x, h]
    if moe:
        nf = wg.shape[1]
        in_specs.append(pl.BlockSpec((tm, LANES), lambda t, f: (t, 0)))
        args.append(gates)
        in_specs += [pl.BlockSpec((None, None, D_MODEL, tf), lambda t, f: (j, f, 0, 0)),
                     pl.BlockSpec((None, None, D_MODEL, tf), lambda t, f: (j, f, 0, 0)),
                     pl.BlockSpec((None, None, tf, D_MODEL), lambda t, f: (j, f, 0, 0))]
    else:
        nf = wg.shape[2] // tf
        in_specs += [pl.BlockSpec((None, D_MODEL, tf), lambda t, f: (j, 0, f)),
                     pl.BlockSpec((None, D_MODEL, tf), lambda t, f: (j, 0, f)),
                     pl.BlockSpec((None, tf, D_MODEL), lambda t, f: (j, f, 0))]
    args += [wg, wu, wd]
    if ple is not None:
        p, W, i = ple
        in_specs += [pl.BlockSpec((None, tm, PLE_DIM), lambda t, f: (i, t, 0)),
                     pl.BlockSpec((None, D_MODEL, D_MODEL), lambda t, f: (i, 0, 0)),
                     pl.BlockSpec((None, PLE_DIM, D_MODEL), lambda t, f: (i, 0, 0))]
        args += [p, W["w_ple_gate"], W["w_ple"]]
    return pl.pallas_call(
        functools.partial(_ffn_kernel, moe=moe, ple=ple is not None),
        grid=(T // tm, nf),
        in_specs=in_specs,
        out_specs=row,
        out_shape=jax.ShapeDtypeStruct((T, D_MODEL), F32),
        scratch_shapes=[pltpu.VMEM((tm, D_MODEL), F32)],
        compiler_params=_params("arbitrary", "arbitrary"),
        name="ffn_moe" if moe else "ffn_dense",
    )(*args)


def _moe_sparse_kernel(dest_ref, w1_ref, w2_ref, tile_e_ref, tile_rows_ref, n_valid_ref,
                       xt_ref, wg_ref, wu_ref, wd_ref, out_ref,
                       src_ref, ws_ref, g_ref, y_ref, *, tb):
    b = pl.program_id(0)
    j = pl.program_id(1)
    n_steps = pl.num_programs(1)
    n_valid = n_valid_ref[b]
    ts = MOE_TILE
    unroll = SUBLANES
    tile8 = lambda r: pl.ds(pl.multiple_of(r * TOKEN_TILES, TOKEN_TILES), TOKEN_TILES)
    rows8 = lambda start: pl.ds(pl.multiple_of(start, TOKEN_TILES), TOKEN_TILES)
    slot_rows = tb * TOKEN_TILES

    def gather_row(tile, buf, r):
        g_ref[buf, tile8(r), :] = xt_ref[rows8(src_ref[tile * ts + r] & (slot_rows - 1)), :]

    def scatter_row(tile, buf, r):
        out_ref[rows8(src_ref[tile * ts + r]), :] = ws_ref[tile * ts + r] * y_ref[buf, tile8(r), :]

    def row_loop(row_fn, tile, buf):
        def group(g, carry):
            for u in range(unroll):
                row_fn(tile, buf, g * unroll + u)
            return carry
        lax.fori_loop(0, ts // unroll, group, 0)

    @pl.when((b == 0) & (j == 0))
    def _():
        g_ref[...] = jnp.zeros_like(g_ref)
        y_ref[...] = jnp.zeros_like(y_ref)

    @pl.when(j == 0)
    def _():
        def fill(t, carry):
            d = dest_ref[b * tb + t]
            d1 = d & 0xFFFF
            d2 = d >> 16
            src_ref[d1] = t * TOKEN_TILES
            src_ref[d2] = t * TOKEN_TILES + slot_rows
            ws_ref[d1] = w1_ref[b * tb + t]
            ws_ref[d2] = w2_ref[b * tb + t]
            return carry

        lax.fori_loop(0, tb, fill, 0, unroll=unroll)

        def pad_tile(tile, carry):
            n_rows = tile_rows_ref[b * n_steps + tile]
            last = tile * ts + n_rows - 1

            def pad_row(r, c):
                src_ref[tile * ts + r] = src_ref[last]
                ws_ref[tile * ts + r] = ws_ref[last]
                return c

            return lax.fori_loop(n_rows, ts, pad_row, carry)

        lax.fori_loop(0, n_valid, pad_tile, 0)
        row_loop(gather_row, 0, 0)

    @pl.when(j < n_valid)
    def _():
        cur = j & 1
        nxt = 1 - cur
        tile_next = jnp.minimum(j + 1, n_valid - 1)
        tile_prev = jnp.maximum(j - 1, 0)
        x = _from_token_layout(g_ref.at[cur], ts).astype(BF16)
        n_chunks = len(MOE_FF_CHUNKS)

        def move_rows(c):
            for r in range(c * ts // n_chunks, (c + 1) * ts // n_chunks):
                gather_row(tile_next, nxt, r)
                scatter_row(tile_prev, nxt, r)

        _to_token_layout(y_ref.at[cur], _swiglu(x, wg_ref, wu_ref, wd_ref, move_rows))

    @pl.when(j == n_valid)
    def _():
        row_loop(scatter_row, n_valid - 1, (n_valid - 1) & 1)


def _moe_plan(meta, cnt, B, L, cb):
    ts = MOE_TILE
    n_steps = TOP_K * L // ts + N_EXPERTS + 1
    m = meta.reshape(B, L // cb, SUBLANES, cb).transpose(2, 0, 1, 3).reshape(SUBLANES, B, L)
    e1, e2, r1, r2 = m[0], m[1], m[2], m[3]
    w1 = lax.bitcast_convert_type(m[4], F32).reshape(B * L)
    w2 = lax.bitcast_convert_type(m[5], F32).reshape(B * L)
    cnt = cnt[:, :, 0]
    padded = (cnt + ts - 1) // ts * ts
    ends = jnp.cumsum(padded, axis=1)
    off = ends - padded
    experts = jnp.arange(N_EXPERTS, dtype=I32)

    def pick(table, idx):
        return jnp.sum(jnp.where(idx[:, :, None] == experts, table[:, None, :], 0), axis=2)

    d1 = pick(off, e1) + r1
    d2 = pick(off, e2) + r2
    dest = (d1 | (d2 << 16)).reshape(B * L).astype(I32)
    start = jnp.arange(n_steps, dtype=I32) * ts
    tile_e = jnp.sum((ends[:, None, :] <= start[None, :, None]).astype(I32), axis=2)
    valid = start[None, :] < ends[:, -1:]
    last_e = jnp.max(jnp.where(valid, tile_e, 0), axis=1, keepdims=True)
    tile_e = jnp.where(valid, tile_e, last_e)
    filled = pick(cnt, tile_e) - (start[None, :] - pick(off, tile_e))
    tile_rows = jnp.where(valid, jnp.clip(filled, 0, ts), 0)
    n_valid = (ends[:, -1] // ts).astype(I32)
    return dest, w1, w2, tile_e.reshape(-1).astype(I32), tile_rows.reshape(-1).astype(I32), n_valid, n_steps


def _moe_sparse(xt, plan, wg, wu, wd, j, B, L):
    dest, w1, w2, tile_e, tile_rows, n_valid, n_steps = plan
    ts = MOE_TILE
    tf = D_FF_TILE
    slot_rows = L * TOKEN_TILES
    assert slot_rows & (slot_rows - 1) == 0, "the kernel masks row offsets with slot_rows - 1"
    tok = pl.BlockSpec((slot_rows, LANES), lambda b, k, *_: (b, 0), pipeline_mode=pl.Buffered(1))
    tok_out = pl.BlockSpec((TOP_K * slot_rows, LANES), lambda b, k, *_: (b, 0), pipeline_mode=pl.Buffered(1))
    expert = lambda b, k, dest, w1, w2, te, tr, nv: (j, te[b * n_steps + k], 0, 0)
    grid_spec = pltpu.PrefetchScalarGridSpec(
        num_scalar_prefetch=6,
        grid=(B, n_steps),
        in_specs=[tok,
                  pl.BlockSpec((None, None, D_MODEL, tf), expert),
                  pl.BlockSpec((None, None, D_MODEL, tf), expert),
                  pl.BlockSpec((None, None, tf, D_MODEL), expert)],
        out_specs=tok_out,
        scratch_shapes=[pltpu.SMEM((n_steps * ts,), I32), pltpu.SMEM((n_steps * ts,), F32),
                        pltpu.VMEM((2, ts * TOKEN_TILES, LANES), F32), pltpu.VMEM((2, ts * TOKEN_TILES, LANES), F32)],
    )
    out = pl.pallas_call(
        functools.partial(_moe_sparse_kernel, tb=L),
        grid_spec=grid_spec,
        out_shape=jax.ShapeDtypeStruct((B * TOP_K * slot_rows, LANES), F32),
        compiler_params=_params("arbitrary", "arbitrary"),
        name="moe_sparse",
    )(dest, w1, w2, tile_e, tile_rows, n_valid, xt, wg, wu, wd)
    return out.reshape(B, TOP_K, slot_rows, LANES)


def _ple_kernel(*refs, final, add_tokens):
    refs = list(refs)
    h_ref = refs.pop(0)
    m_ref = refs.pop(0) if add_tokens else None
    p_ref, wg_ref, wp_ref = refs[:3]
    fn_ref = refs[3] if final else None
    out_ref = refs[-1]
    h = h_ref[...]
    if add_tokens:
        for s in range(TOP_K):
            h = h + _from_token_layout(m_ref.at[s], h.shape[0])
    h3 = _ple_update(h, p_ref, wg_ref, wp_ref)
    out_ref[...] = _rms(h3, fn_ref[...]) if final else h3


def _ple(h, m_t, p, W, i, tm):
    T = h.shape[0]
    final = i == DEPTH - 1
    add_tokens = m_t is not None
    rows = pl.BlockSpec((tm, D_MODEL), lambda t: (t, 0))
    in_specs = [rows]
    args = [h]
    if add_tokens:
        per_seq = m_t.shape[2] // (tm * TOKEN_TILES)
        in_specs.append(pl.BlockSpec((None, TOP_K, tm * TOKEN_TILES, LANES),
                                     lambda t: (t // per_seq, 0, t % per_seq, 0)))
        args.append(m_t)
    in_specs += [pl.BlockSpec((None, tm, PLE_DIM), lambda t: (i, t, 0)),
                 pl.BlockSpec((None, D_MODEL, D_MODEL), lambda t: (i, 0, 0)),
                 pl.BlockSpec((None, PLE_DIM, D_MODEL), lambda t: (i, 0, 0))]
    args += [p, W["w_ple_gate"], W["w_ple"]]
    if final:
        in_specs.append(pl.BlockSpec((1, D_MODEL), lambda t: (0, 0)))
        args.append(W["final_norm"])
    return pl.pallas_call(
        functools.partial(_ple_kernel, final=final, add_tokens=add_tokens),
        grid=(T // tm,),
        in_specs=in_specs,
        out_specs=rows,
        out_shape=jax.ShapeDtypeStruct((T, D_MODEL), F32),
        compiler_params=_params("arbitrary"),
        name="ple_final" if final else "ple",
    )(*args)


def _rope_tables(pos):
    half = RET_HEAD_DIM // 2
    inv = ROPE_THETA ** (-np.arange(half, dtype=np.float64) / half)
    ang = np.asarray(pos, np.float64)[:, None] * inv[None, :]
    cos, sin = np.cos(ang), np.sin(ang)
    reps = LANES // RET_HEAD_DIM
    return (jnp.asarray(np.tile(np.concatenate([cos, cos], axis=-1), (1, reps)), F32),
            jnp.asarray(np.tile(np.concatenate([-sin, sin], axis=-1), (1, reps)), F32))


def _log_decay():
    return np.log1p(-np.power(2.0, -5.0 - np.arange(RET_HEADS, dtype=np.float64)))


def _retention_tables(C):
    log_g = _log_decay()
    idx = np.arange(C, dtype=np.float64)
    rel = idx[:, None] - idx[None, :]
    dmask = np.where(rel >= 0, np.exp(log_g[:, None, None] * np.maximum(rel, 0.0)), 0.0)
    xi = np.exp(log_g[None, :] * (idx[:, None] + 1.0))
    zeta = np.exp(log_g[None, :] * (C - 1.0 - idx)[:, None])
    spread = lambda t: jnp.asarray(np.repeat(t, RET_HEAD_DIM, axis=1), F32)
    return {"dmask": jnp.asarray(dmask, F32), "xi": spread(xi), "zeta": spread(zeta), "decay_c": _chunk_decay(C)}


def _chunk_decay(C):
    return tuple(math.exp(lg * C) for lg in _log_decay())


def _choose_tile(T, cap):
    tm = min(T, cap)
    while T % tm:
        tm //= 2
    return tm


def _layer_prompt(i, h, p, W, rope, tabs, B, L):
    T = B * L
    tm = _choose_tile(T, 512)
    cb = _choose_tile(L, 256)
    j = i // 2
    z = _in_proj(h, W["norm1"], W["w_in"], i, rope[0], rope[1], _choose_tile(L, tm))
    if i % 2 == 0:
        assert i < DEPTH - 1
        h1, hn2, pool, ret = _mixer_prompt(z, h, tabs, W, i, B, L, cb)
        return _ffn(hn2, h1, None, W["ffn_g"], W["ffn_u"], W["ffn_d"], j, tm, ple=(p, W, i)), pool, ret
    h1, xt, meta, cnt, pool, ret = _mixer_prompt(z, h, tabs, W, i, B, L, cb)
    m_t = _moe_sparse(xt, _moe_plan(meta, cnt, B, L, cb), W["moe_g"], W["moe_u"], W["moe_d"], j, B, L)
    return _ple(h1, m_t, p, W, i, tm), pool, ret


def _layer_sample(i, h, p, W, rope, decay_1, buf, s0):
    B = h.shape[0]
    j = i // 2
    z = _in_proj(h, W["norm1"], W["w_in"], i, rope[0], rope[1], B)
    h1, hn2, gates, pool, ret = _mixer_sample(z, h, buf, s0, decay_1, W, i, _choose_tile(B, 16))
    if i % 2 == 0:
        h2 = _ffn(hn2, h1, None, W["ffn_g"], W["ffn_u"], W["ffn_d"], j, B)
    else:
        h2 = _ffn(hn2, h1, gates, W["moe_g"], W["moe_u"], W["moe_d"], j, B)
    return _ple(h2, None, p, W, i, B), pool, ret


def _prepare_weights(norm1, w_in, w_pool, pool_scale, w_o, norm2, ffn_w_gate, ffn_w_up, ffn_w_down,
                     moe_router, moe_w_gate, moe_w_up, moe_w_down, w_ple, w_ple_gate, final_norm):
    r_hi, r_lo = _split_bf16(jnp.pad(moe_router, ((0, 0), (0, 0), (0, LANES - N_EXPERTS))))
    rt_hi, rt_lo = _split_bf16(moe_router.transpose(0, 2, 1))
    return {
        "norm1": norm1[:, None, :], "norm2": norm2[:, None, :], "final_norm": final_norm[None, :],
        "pool_scale": pool_scale[:, None, :],
        "w_in": w_in.astype(BF16), "w_pool": w_pool.astype(BF16), "w_o": w_o.astype(BF16),
        "ffn_g": ffn_w_gate.astype(BF16), "ffn_u": ffn_w_up.astype(BF16), "ffn_d": ffn_w_down.astype(BF16),
        "moe_g": moe_w_gate.astype(BF16), "moe_u": moe_w_up.astype(BF16), "moe_d": moe_w_down.astype(BF16),
        "router_hi": r_hi, "router_lo": r_lo, "router_t_hi": rt_hi, "router_t_lo": rt_lo,
        "w_ple": w_ple.astype(BF16), "w_ple_gate": w_ple_gate.astype(BF16),
    }


def _trunk(x_prompt, x_sample, state_pool, state_ret, p_prompt, p_sample, W):
    B, L, _ = x_prompt.shape
    Bs = x_sample.shape[0]
    hp = x_prompt.reshape(B * L, D_MODEL)
    hs = x_sample.reshape(Bs, D_MODEL)
    pp = p_prompt.reshape(DEPTH, B * L, PLE_DIM)
    ps = p_sample.reshape(DEPTH, Bs, PLE_DIM)
    rope_p = _rope_tables(np.arange(L))
    rope_s = _rope_tables(PAST_LEN + np.arange(1))
    tabs = _retention_tables(_choose_tile(L, 256))
    decay_1 = _chunk_decay(1)
    pool_p, ret_p, pool_s, ret_s = [], [], [], []
    for i in range(DEPTH):
        hp, bp, sp = _layer_prompt(i, hp, pp, W, rope_p, tabs, B, L)
        hs, bs, ss = _layer_sample(i, hs, ps, W, rope_s, decay_1,
                                   state_pool[i].reshape(Bs, POOL_BUF * POOL_WIDTH), state_ret[i])
        pool_p.append(bp)
        ret_p.append(sp)
        pool_s.append(bs)
        ret_s.append(ss)
    return (hp.reshape(B, L, D_MODEL), hs.reshape(Bs, 1, D_MODEL), jnp.stack(pool_p), jnp.stack(ret_p),
            jnp.stack(pool_s).reshape(DEPTH, Bs, POOL_BUF, POOL_WIDTH), jnp.stack(ret_s))


def kernel(x_prompt, x_sample, state_pool, state_ret, p_prompt, p_sample, norm1, w_in, w_pool, pool_scale, w_o, norm2, ffn_w_gate, ffn_w_up, ffn_w_down, moe_router, moe_w_gate, moe_w_up, moe_w_down, w_ple, w_ple_gate, final_norm):
    W = _prepare_weights(norm1, w_in, w_pool, pool_scale, w_o, norm2, ffn_w_gate, ffn_w_up, ffn_w_down,
                         moe_router, moe_w_gate, moe_w_up, moe_w_down, w_ple, w_ple_gate, final_norm)
    return _trunk(x_prompt, x_sample, state_pool, state_ret, p_prompt, p_sample, W)
```

```python
import functools
import math

import numpy as np

import jax
import jax.numpy as jnp
from jax import lax
from jax.experimental import pallas as pl
from jax.experimental.pallas import tpu as pltpu

F32 = jnp.float32
BF16 = jnp.bfloat16
I32 = jnp.int32

D_MODEL = 1024
DEPTH = 4
PAST_LEN = 16384
POOL_WIDTH = 512
POOL_WINDOWS = (2, 4, 8, 16)
POOL_GROUP_DIM = 128
POOL_BUF = 15
RET_HEADS = 8
RET_HEAD_DIM = 64
RET_WIDTH = 512
ROPE_THETA = 10000.0
IN_WIDTH = POOL_WIDTH + 4 * RET_WIDTH
SECTION = 512
D_FF_TILE = 1408
N_EXPERTS = 8
TOP_K = 2
PLE_DIM = 256
NORM_EPS = 1e-6
LANES = 128
SUBLANES = 8
TOKEN_TILES = D_MODEL // LANES
assert TOKEN_TILES == SUBLANES
TAIL_ROWS = 16
MOE_TILE = 256
MOE_FF_CHUNKS = ((0, 512), (512, 1024), (1024, D_FF_TILE))
VMEM_LIMIT = 58 * 1024 * 1024


def _rms(x, g):
    return x * lax.rsqrt(jnp.mean(x * x, axis=-1, keepdims=True) + NORM_EPS) * g


def _dot(a, b):
    return jnp.dot(a, b, preferred_element_type=F32)


def _dot_nt(a, b):
    return lax.dot_general(a, b, (((1,), (1,)), ((), ())), preferred_element_type=F32)


def _dot_tn(a, b):
    return lax.dot_general(a, b, (((0,), (0,)), ((), ())), preferred_element_type=F32)


def _params(*sem):
    return pltpu.CompilerParams(dimension_semantics=sem, vmem_limit_bytes=VMEM_LIMIT)


def _split_bf16(x):
    hi = x.astype(BF16)
    return hi, (x - hi.astype(F32)).astype(BF16)


def _to_token_layout(ref, x):
    n = x.shape[0]
    for j in range(TOKEN_TILES):
        ref[pl.ds(j, n, stride=TOKEN_TILES), :] = x[:, j * LANES:(j + 1) * LANES]


def _from_token_layout(ref, n):
    return jnp.concatenate([ref[pl.ds(j, n, stride=TOKEN_TILES), :] for j in range(TOKEN_TILES)], axis=1)


def _rope_slab(x, cos, sin, first_half):
    fwd = pltpu.roll(x, 32, 1)
    bwd = pltpu.roll(x, LANES - 32, 1)
    return x * cos + jnp.where(first_half, bwd, fwd) * sin


def _in_proj_kernel(h_ref, n1_ref, w_ref, cos_ref, sin_ref, z_ref):
    hn = _rms(h_ref[...], n1_ref[...]).astype(BF16)
    cos = cos_ref[...]
    sin = sin_ref[...]
    lane = lax.broadcasted_iota(I32, (hn.shape[0], LANES), 1)
    first_half = (lane & 32) == 0
    for s in range(IN_WIDTH // SECTION):
        zs = _dot(hn, w_ref[:, s * SECTION:(s + 1) * SECTION])
        if s in (1, 2):
            scale = 1.0 if s == 1 else RET_HEAD_DIM ** -0.5
            for c in range(SECTION // LANES):
                slab = _rope_slab(zs[:, c * LANES:(c + 1) * LANES], cos, sin, first_half)
                z_ref[:, s * SECTION + c * LANES:s * SECTION + (c + 1) * LANES] = slab * scale
        else:
            z_ref[:, s * SECTION:(s + 1) * SECTION] = zs


def _in_proj(h, n1, w_in, i, cos_t, sin_t, tm):
    T = h.shape[0]
    n_tab = cos_t.shape[0] // tm if cos_t.shape[0] > 1 else 1
    tb = tm if cos_t.shape[0] > 1 else 1
    return pl.pallas_call(
        _in_proj_kernel,
        grid=(T // tm,),
        in_specs=[
            pl.BlockSpec((tm, D_MODEL), lambda t: (t, 0)),
            pl.BlockSpec((None, 1, D_MODEL), lambda t: (i, 0, 0)),
            pl.BlockSpec((None, D_MODEL, IN_WIDTH), lambda t: (i, 0, 0)),
            pl.BlockSpec((tb, LANES), lambda t: (t % n_tab, 0)),
            pl.BlockSpec((tb, LANES), lambda t: (t % n_tab, 0)),
        ],
        out_specs=pl.BlockSpec((tm, IN_WIDTH), lambda t: (t, 0)),
        out_shape=jax.ShapeDtypeStruct((T, IN_WIDTH), F32),
        compiler_params=_params("arbitrary"),
        name="in_proj",
    )(h, n1, w_in, cos_t, sin_t)


def _top2(lg, idx, axis, n_idx):
    neg = jnp.float32(-jnp.inf)
    m1 = jnp.max(lg, axis=axis, keepdims=True)
    i1 = jnp.min(jnp.where(lg == m1, idx, n_idx), axis=axis, keepdims=True)
    lg2 = jnp.where(idx == i1, neg, lg)
    m2 = jnp.max(lg2, axis=axis, keepdims=True)
    i2 = jnp.min(jnp.where(lg2 == m2, idx, n_idx), axis=axis, keepdims=True)
    e2 = jnp.exp(m2 - m1)
    den = 1.0 + e2
    return i1, i2, 1.0 / den, e2 / den


def _route(x, rhi_ref, rlo_ref):
    x_hi, x_lo = _split_bf16(x)
    logits = _dot(x_hi, rhi_ref[...]) + _dot(x_lo, rhi_ref[...]) + _dot(x_hi, rlo_ref[...])
    lane = lax.broadcasted_iota(I32, logits.shape, 1)
    lg = jnp.where(lane < N_EXPERTS, logits, jnp.float32(-jnp.inf))
    i1, i2, w1, w2 = _top2(lg, lane, 1, LANES)
    return jnp.where(lane == i1, w1, 0.0) + jnp.where(lane == i2, w2, 0.0)


def _route_logits_t(x, rthi_ref, rtlo_ref):
    x_hi, x_lo = _split_bf16(x)
    return _dot_nt(rthi_ref[...], x_hi) + _dot_nt(rthi_ref[...], x_lo) + _dot_nt(rtlo_ref[...], x_hi)


def _route_ranks(lt, fresh, live, carry_ref, meta_ref, cnt_ref):
    n = lt.shape[1]
    sub = lax.broadcasted_iota(I32, lt.shape, 0)
    i1, i2, w1, w2 = _top2(lt, sub, 0, N_EXPERTS)
    ind = jnp.where(sub == i1, 1.0, 0.0) + jnp.where(sub == i2, 1.0, 0.0)
    tri = jnp.where(lax.broadcasted_iota(I32, (n, n), 0) <= lax.broadcasted_iota(I32, (n, n), 1), 1.0, 0.0)
    csum = _dot(ind.astype(BF16), tri.astype(BF16))
    carry = carry_ref[...] * (1.0 - fresh)
    rank = carry[:, 0:1] + csum - ind
    r1 = jnp.sum(jnp.where(sub == i1, rank, 0.0), axis=0, keepdims=True).astype(I32)
    r2 = jnp.sum(jnp.where(sub == i2, rank, 0.0), axis=0, keepdims=True).astype(I32)
    carry_ref[...] = carry + csum[:, n - 1:n] * live
    cnt_ref[0] = carry_ref[...].astype(I32)
    rows = [i1, i2, r1, r2, pltpu.bitcast(w1, I32), pltpu.bitcast(w2, I32)]
    meta = jnp.zeros(lt.shape, I32)
    for j, rowv in enumerate(rows):
        meta = jnp.where(sub == j, jnp.broadcast_to(rowv, lt.shape), meta)
    meta_ref[0] = meta


def _pool_project(d_slabs, u, wpool_ref, pscale_ref):
    ys = []
    for g in range(len(POOL_WINDOWS)):
        sl = slice(g * POOL_GROUP_DIM, (g + 1) * POOL_GROUP_DIM)
        ys.append(_dot((d_slabs[g] - u[:, sl]).astype(BF16), wpool_ref[g]) * pscale_ref[:, sl])
    return ys


def _mixer_prompt_kernel(*refs, cb, nc, n_chunks, decay_c, moe):
    (u_ref, q_ref, k_ref, v_ref, g_ref, h_ref, dmask_ref, xi_ref, zeta_ref,
     wpool_ref, pscale_ref, wo_ref, n2_ref) = refs[:13]
    if moe:
        (rthi_ref, rtlo_ref, h1_ref, xt_ref, meta_ref, cnt_ref, pool_out_ref, ret_out_ref,
         s_scr, p_scr, mix_scr, carry_scr) = refs[13:]
    else:
        h1_ref, hn2_ref, pool_out_ref, ret_out_ref, s_scr, p_scr, mix_scr = refs[13:]
    step = pl.program_id(0)
    c = jnp.minimum(step, n_chunks - 1) % nc
    c_tail = jnp.maximum(step - 1, 0) % nc

    @pl.when(step == 0)
    def _():
        mix_scr[...] = jnp.zeros_like(mix_scr)
        if moe:
            carry_scr[...] = jnp.zeros_like(carry_scr)

    @pl.when(c == 0)
    def _():
        s_scr[...] = jnp.zeros_like(s_scr)
        p_scr[0:TAIL_ROWS, :] = jnp.zeros((TAIL_ROWS, POOL_WIDTH), F32)

    proj = _dot(mix_scr[...], wo_ref[...])

    u = u_ref[...]
    old_tail = p_scr[0:TAIL_ROWS, :]
    p_scr[TAIL_ROWS:TAIL_ROWS + cb, :] = u
    pos = c * cb + lax.broadcasted_iota(I32, (cb, POOL_GROUP_DIM), 0)
    means = []
    for g, w in enumerate(POOL_WINDOWS):
        sl = slice(g * POOL_GROUP_DIM, (g + 1) * POOL_GROUP_DIM)
        s = u[:, sl]
        for j in range(1, w):
            s = s + p_scr[TAIL_ROWS - j:TAIL_ROWS - j + cb, sl]
        means.append(s / jnp.minimum(pos + 1, w).astype(F32))
    ys = _pool_project(means, u, wpool_ref, pscale_ref)
    new_tail = p_scr[cb:cb + TAIL_ROWS, :]

    h1 = h_ref[...] + proj
    h1_ref[...] = h1
    hn2 = _rms(h1, n2_ref[...])

    q = q_ref[...]
    k = k_ref[...]
    qb = q.astype(BF16)
    kb = k.astype(BF16)
    kz = (k * zeta_ref[...]).astype(BF16)
    vb = v_ref[...].astype(BF16)
    xi = xi_ref[...]
    heads = range(RET_HEADS)
    sls = [slice(hd * RET_HEAD_DIM, (hd + 1) * RET_HEAD_DIM) for hd in heads]
    s_prev = [s_scr[hd] for hd in heads]
    sc = [_dot_nt(qb[:, sl], kb[:, sl]) for sl in sls]
    cross = [_dot(qb[:, sl], s_prev[hd].astype(BF16)) for hd, sl in enumerate(sls)]
    upd = [_dot_tn(kz[:, sl], vb[:, sl]) for sl in sls]

    if moe:
        _to_token_layout(xt_ref, hn2)
        lt = _route_logits_t(hn2, rthi_ref, rtlo_ref)
    else:
        hn2_ref[...] = hn2.astype(BF16)

    pm = [(sc[hd] * dmask_ref[hd]).astype(BF16) for hd in heads]
    o_h = [_dot(pm[hd], vb[:, sl]) + cross[hd] * xi[:, sl] for hd, sl in enumerate(sls)]
    new_s = [s_prev[hd] * decay_c[hd] + upd[hd] for hd in heads]

    if moe:
        _route_ranks(lt, jnp.where(c_tail == 0, 1.0, 0.0), jnp.where(step > 0, 1.0, 0.0), carry_scr, meta_ref, cnt_ref)

    mu = [jnp.mean(o, axis=-1, keepdims=True) for o in o_h]
    cen = [o - m for o, m in zip(o_h, mu)]
    var = [jnp.mean(jnp.square(x), axis=-1, keepdims=True) for x in cen]
    os_ = [x * lax.rsqrt(v + NORM_EPS) for x, v in zip(cen, var)]
    o = jnp.concatenate(os_, axis=1) * jax.nn.silu(g_ref[...])
    mix_scr[...] = jnp.concatenate(ys + [o], axis=1).astype(BF16)

    real = step < n_chunks
    new_tail = jnp.where(real, new_tail, old_tail)
    p_scr[0:TAIL_ROWS, :] = new_tail
    pool_out_ref[0] = new_tail[TAIL_ROWS - POOL_BUF:, :]
    for hd in heads:
        kept = jnp.where(real, new_s[hd], s_prev[hd])
        s_scr[hd] = kept
        ret_out_ref[0, hd] = kept


def _mixer_prompt(z, h, tabs, W, i, B, L, cb):
    T = B * L
    nc = L // cb
    moe = i % 2 == 1
    n_chunks = B * nc
    head = lambda g: jnp.minimum(g, n_chunks - 1)
    tail = lambda g: jnp.maximum(g - 1, 0)
    zspec = lambda s: pl.BlockSpec((cb, SECTION), lambda g, s=s: (head(g), s))
    const2 = lambda shape: pl.BlockSpec(shape, lambda g: (0, 0))
    const3 = lambda shape: pl.BlockSpec(shape, lambda g: (0, 0, 0))
    layer3 = lambda shape: pl.BlockSpec((None,) + shape, lambda g: (i, 0, 0))
    rows = pl.BlockSpec((cb, D_MODEL), lambda g: (tail(g), 0))
    in_specs = [zspec(0), zspec(1), zspec(2), zspec(3), zspec(4), rows,
                const3((RET_HEADS, cb, cb)), const2((cb, RET_WIDTH)), const2((cb, RET_WIDTH)),
                pl.BlockSpec((None, len(POOL_WINDOWS), POOL_GROUP_DIM, POOL_GROUP_DIM), lambda g: (i, 0, 0, 0)),
                layer3((1, POOL_WIDTH)), layer3((D_MODEL, D_MODEL)), layer3((1, D_MODEL))]
    args = [z, z, z, z, z, h, tabs["dmask"], tabs["xi"], tabs["zeta"],
            W["w_pool"], W["pool_scale"], W["w_o"], W["norm2"]]
    scratch = [pltpu.VMEM((RET_HEADS, RET_HEAD_DIM, RET_HEAD_DIM), F32),
               pltpu.VMEM((TAIL_ROWS + cb, POOL_WIDTH), F32),
               pltpu.VMEM((cb, D_MODEL), BF16)]
    state_specs = [pl.BlockSpec((1, POOL_BUF, POOL_WIDTH), lambda g: (head(g) // nc, 0, 0)),
                   pl.BlockSpec((1, RET_HEADS, RET_HEAD_DIM, RET_HEAD_DIM), lambda g: (head(g) // nc, 0, 0, 0))]
    state_shapes = [jax.ShapeDtypeStruct((B, POOL_BUF, POOL_WIDTH), F32),
                    jax.ShapeDtypeStruct((B, RET_HEADS, RET_HEAD_DIM, RET_HEAD_DIM), F32)]
    if moe:
        j = i // 2
        rt = pl.BlockSpec((None, N_EXPERTS, D_MODEL), lambda g: (j, 0, 0))
        in_specs += [rt, rt]
        args += [W["router_t_hi"], W["router_t_lo"]]
        out_specs = [rows,
                     pl.BlockSpec((cb * TOKEN_TILES, LANES), lambda g: (tail(g), 0)),
                     pl.BlockSpec((1, SUBLANES, cb), lambda g: (tail(g), 0, 0)),
                     pl.BlockSpec((1, N_EXPERTS, LANES), lambda g: (tail(g) // nc, 0, 0))]
        out_shape = [jax.ShapeDtypeStruct((T, D_MODEL), F32),
                     jax.ShapeDtypeStruct((T * TOKEN_TILES, LANES), F32),
                     jax.ShapeDtypeStruct((B * nc, SUBLANES, cb), I32),
                     jax.ShapeDtypeStruct((B, N_EXPERTS, LANES), I32)]
        scratch.append(pltpu.VMEM((N_EXPERTS, LANES), F32))
    else:
        out_specs = [rows, rows]
        out_shape = [jax.ShapeDtypeStruct((T, D_MODEL), F32), jax.ShapeDtypeStruct((T, D_MODEL), BF16)]
    return pl.pallas_call(
        functools.partial(_mixer_prompt_kernel, cb=cb, nc=nc, n_chunks=n_chunks, decay_c=tabs["decay_c"], moe=moe),
        grid=(n_chunks + 1,),
        in_specs=in_specs,
        out_specs=out_specs + state_specs,
        out_shape=out_shape + state_shapes,
        scratch_shapes=scratch,
        compiler_params=_params("arbitrary"),
        name="mixer_prompt",
    )(*args)


def _to_head_rows(ref, x):
    n = x.shape[0]
    for hd in range(RET_HEADS):
        ref[pl.ds(hd, n, stride=RET_HEADS), 0:RET_HEAD_DIM] = x[:, hd * RET_HEAD_DIM:(hd + 1) * RET_HEAD_DIM]


def _mixer_sample_kernel(*refs, bb, decay_1, moe):
    (u_ref, q_ref, k_ref, v_ref, g_ref, h_ref, buf_ref, s_ref,
     wpool_ref, pscale_ref, wo_ref, n2_ref) = refs[:12]
    n_in = 14 if moe else 12
    router_refs = refs[12:n_in]
    h1_ref, hn2_ref = refs[n_in:n_in + 2]
    gates_ref = refs[n_in + 2] if moe else None
    buf_out_ref, s_out_ref, q2_scr, k2_scr, v2_scr, g2_scr, c2_scr, o2_scr, qrep_scr, krep_scr = refs[-10:]
    dh = RET_HEAD_DIM
    grp = SUBLANES

    u = u_ref[...]
    means = []
    for g, w in enumerate(POOL_WINDOWS):
        s = u[:, g * POOL_GROUP_DIM:(g + 1) * POOL_GROUP_DIM]
        for j in range(1, w):
            lo = (POOL_BUF - j) * POOL_WIDTH + g * POOL_GROUP_DIM
            s = s + buf_ref[:, lo:lo + POOL_GROUP_DIM]
        means.append(s / float(min(PAST_LEN + 1, w)))
    ys = _pool_project(means, u, wpool_ref, pscale_ref)
    keep = (POOL_BUF - 1) * POOL_WIDTH
    buf_out_ref[:, 0:keep] = buf_ref[:, POOL_WIDTH:POOL_WIDTH + keep]
    buf_out_ref[:, keep:keep + POOL_WIDTH] = u

    for ref, src in ((q2_scr, q_ref), (k2_scr, k_ref), (v2_scr, v_ref), (g2_scr, g_ref)):
        _to_head_rows(ref, src[...])
    spread = jnp.where(lax.broadcasted_iota(I32, (grp, grp * LANES), 0)
                       == lax.broadcasted_iota(I32, (grp, grp * LANES), 1) // LANES, 1.0, 0.0)

    def per_group(gi, carry):
        rows = pl.ds(pl.multiple_of(gi * grp, grp), grp)
        qrep_scr[...] = _dot_tn(q_ref[rows, :], spread)
        krep_scr[...] = _dot_tn(k_ref[rows, :], spread)
        for bl in range(grp):
            b = gi * grp + bl
            cross = []
            for hd in range(RET_HEADS):
                tile = (slice(hd * dh, (hd + 1) * dh), slice(bl * LANES, bl * LANES + dh))
                s_prev = s_ref[b, hd]
                cross.append(jnp.sum(s_prev * qrep_scr[tile], axis=0, keepdims=True))
                v_row = v2_scr[pl.ds(b * RET_HEADS + hd, 1), 0:dh]
                s_out_ref[b, hd] = s_prev * decay_1[hd] + krep_scr[tile] * v_row
            c2_scr[pl.ds(pl.multiple_of(b * RET_HEADS, RET_HEADS), RET_HEADS), 0:dh] = jnp.concatenate(cross, axis=0)
        return carry

    lax.fori_loop(0, bb // grp, per_group, 0)

    n2 = bb * RET_HEADS
    q2, k2, v2 = q2_scr[:, 0:dh], k2_scr[:, 0:dh], v2_scr[:, 0:dh]
    head = lax.broadcasted_iota(I32, (n2, dh), 0) % RET_HEADS
    gamma = jnp.zeros((n2, dh), F32)
    for hd in range(RET_HEADS):
        gamma = jnp.where(head == hd, decay_1[hd], gamma)
    o2 = jnp.sum(q2 * k2, axis=-1, keepdims=True) * v2 + gamma * c2_scr[:, 0:dh]
    mu = jnp.mean(o2, axis=-1, keepdims=True)
    var = jnp.mean(jnp.square(o2 - mu), axis=-1, keepdims=True)
    o2_scr[:, 0:dh] = (o2 - mu) * lax.rsqrt(var + NORM_EPS) * jax.nn.silu(g2_scr[:, 0:dh])
    o = [o2_scr[pl.ds(hd, bb, stride=RET_HEADS), 0:dh] for hd in range(RET_HEADS)]
    mix = jnp.concatenate(ys + o, axis=1)
    h1 = h_ref[...] + _dot(mix.astype(BF16), wo_ref[...])
    h1_ref[...] = h1
    hn2 = _rms(h1, n2_ref[...])
    hn2_ref[...] = hn2.astype(BF16)
    if moe:
        gates_ref[...] = _route(hn2, *router_refs)


def _mixer_sample(z, h, buf, s0, decay_1, W, i, bb):
    B = h.shape[0]
    moe = i % 2 == 1
    flat = POOL_BUF * POOL_WIDTH
    zspec = lambda s: pl.BlockSpec((bb, SECTION), lambda t, s=s: (t, s))
    layer3 = lambda shape: pl.BlockSpec((None,) + shape, lambda t: (i, 0, 0))
    rows = pl.BlockSpec((bb, D_MODEL), lambda t: (t, 0))
    pool = pl.BlockSpec((bb, flat), lambda t: (t, 0))
    state = pl.BlockSpec((bb, RET_HEADS, RET_HEAD_DIM, RET_HEAD_DIM), lambda t: (t, 0, 0, 0))
    in_specs = [zspec(0), zspec(1), zspec(2), zspec(3), zspec(4), rows, pool, state,
                pl.BlockSpec((None, len(POOL_WINDOWS), POOL_GROUP_DIM, POOL_GROUP_DIM), lambda t: (i, 0, 0, 0)),
                layer3((1, POOL_WIDTH)), layer3((D_MODEL, D_MODEL)), layer3((1, D_MODEL))]
    args = [z, z, z, z, z, h, buf, s0, W["w_pool"], W["pool_scale"], W["w_o"], W["norm2"]]
    out_specs = [rows, rows]
    out_shape = [jax.ShapeDtypeStruct((B, D_MODEL), F32), jax.ShapeDtypeStruct((B, D_MODEL), BF16)]
    if moe:
        j = i // 2
        r = pl.BlockSpec((None, D_MODEL, LANES), lambda t: (j, 0, 0))
        in_specs += [r, r]
        args += [W["router_hi"], W["router_lo"]]
        out_specs.append(pl.BlockSpec((bb, LANES), lambda t: (t, 0)))
        out_shape.append(jax.ShapeDtypeStruct((B, LANES), F32))
    out_specs += [pool, state]
    out_shape += [jax.ShapeDtypeStruct((B, flat), F32), jax.ShapeDtypeStruct(s0.shape, F32)]
    outs = pl.pallas_call(
        functools.partial(_mixer_sample_kernel, bb=bb, decay_1=decay_1, moe=moe),
        grid=(B // bb,),
        in_specs=in_specs,
        out_specs=out_specs,
        out_shape=out_shape,
        scratch_shapes=[pltpu.VMEM((bb * RET_HEADS, LANES), F32) for _ in range(6)]
        + [pltpu.VMEM((RET_WIDTH, SUBLANES * LANES), F32) for _ in range(2)],
        compiler_params=_params("arbitrary"),
        name="mixer_sample",
    )(*args)
    if moe:
        h1, hn2, gates, pool, ret = outs
    else:
        (h1, hn2, pool, ret), gates = outs, None
    return h1, hn2, gates, pool, ret


def _swiglu(x, wg_ref, wu_ref, wd_ref, between_chunks=None):
    y = None
    for c, (lo, hi) in enumerate(MOE_FF_CHUNKS):
        if between_chunks is not None:
            between_chunks(c)
        hidden = jax.nn.silu(_dot(x, wg_ref[:, lo:hi])) * _dot(x, wu_ref[:, lo:hi])
        part = _dot(hidden.astype(BF16), wd_ref[lo:hi, :])
        y = part if y is None else y + part
    return y


def _ple_update(h, p_ref, wg_ref, wp_ref):
    gate = jax.nn.sigmoid(_dot(h.astype(BF16), wg_ref[...]))
    return h + gate * _dot(p_ref[...].astype(BF16), wp_ref[...])


def _ffn_kernel(*refs, moe, ple, cast_weights):
    refs = list(refs)
    x_ref, h_ref = refs[:2]
    del refs[:2]
    gates_ref = refs.pop(0) if moe else None
    wg_ref, wu_ref, wd_ref = refs[:3]
    ple_refs = refs[3:6] if ple else None
    acc_ref = refs[-1]
    if cast_weights:
        out_ref, wg_out, wu_out, wd_out = refs[-5:-1]
        for src, dst in ((wg_ref, wg_out), (wu_ref, wu_out), (wd_ref, wd_out)):
            dst[...] = src[...].astype(BF16)
        wg_ref, wu_ref, wd_ref = wg_out, wu_out, wd_out
    else:
        out_ref = refs[-2]
    f = pl.program_id(1)
    y = _swiglu(x_ref[...], wg_ref, wu_ref, wd_ref)
    if moe:
        gates = gates_ref[...]
        lane = lax.broadcasted_iota(I32, gates.shape, 1)
        y = y * jnp.sum(jnp.where(lane == f, gates, 0.0), axis=-1, keepdims=True)

    @pl.when(f == 0)
    def _():
        acc_ref[...] = y

    @pl.when(f > 0)
    def _():
        acc_ref[...] += y

    @pl.when(f == pl.num_programs(1) - 1)
    def _():
        h2 = h_ref[...] + acc_ref[...]
        out_ref[...] = _ple_update(h2, *ple_refs) if ple else h2


def _ffn(x, h, gates, wg, wu, wd, j, tm, ple=None, cast_weights=False):
    T = x.shape[0]
    moe = gates is not None
    tf = D_FF_TILE
    row = pl.BlockSpec((tm, D_MODEL), lambda t, f: (t, 0))
    in_specs = [row, row]
    args = [x, h]
    if moe:
        nf = wg.shape[1]
        in_specs.append(pl.BlockSpec((tm, LANES), lambda t, f: (t, 0)))
        args.append(gates)
        w_blocks = [((None, None, D_MODEL, tf), lambda jj: lambda t, f: (jj, f, 0, 0))] * 2
        w_blocks.append(((None, None, tf, D_MODEL), lambda jj: lambda t, f: (jj, f, 0, 0)))
    else:
        nf = wg.shape[2] // tf
        w_blocks = [((None, D_MODEL, tf), lambda jj: lambda t, f: (jj, 0, f))] * 2
        w_blocks.append(((None, tf, D_MODEL), lambda jj: lambda t, f: (jj, f, 0)))
    in_specs += [pl.BlockSpec(shape, index(j)) for shape, index in w_blocks]
    args += [wg, wu, wd]
    if ple is not None:
        p, W, i = ple
        in_specs += [pl.BlockSpec((None, tm, PLE_DIM), lambda t, f: (i, t, 0)),
                     pl.BlockSpec((None, D_MODEL, D_MODEL), lambda t, f: (i, 0, 0)),
                     pl.BlockSpec((None, PLE_DIM, D_MODEL), lambda t, f: (i, 0, 0))]
        args += [p, W["w_ple_gate"], W["w_ple"]]
    out_specs = row
    out_shape = jax.ShapeDtypeStruct((T, D_MODEL), F32)
    if cast_weights:
        assert T == tm, "each weight block must be visited exactly once"
        out_specs = [row] + [pl.BlockSpec(shape, index(0)) for shape, index in w_blocks]
        out_shape = [out_shape] + [jax.ShapeDtypeStruct((1,) + w.shape[1:], BF16) for w in (wg, wu, wd)]
    return pl.pallas_call(
        functools.partial(_ffn_kernel, moe=moe, ple=ple is not None, cast_weights=cast_weights),
        grid=(T // tm, nf),
        in_specs=in_specs,
        out_specs=out_specs,
        out_shape=out_shape,
        scratch_shapes=[pltpu.VMEM((tm, D_MODEL), F32)],
        compiler_params=_params("arbitrary", "arbitrary"),
        name="ffn_moe" if moe else "ffn_dense",
    )(*args)


def _moe_sparse_kernel(dest_ref, w1_ref, w2_ref, tile_e_ref, tile_rows_ref, n_valid_ref,
                       xt_ref, wg_ref, wu_ref, wd_ref, out_ref,
                       src_ref, ws_ref, g_ref, y_ref, *, tb):
    b = pl.program_id(0)
    j = pl.program_id(1)
    n_steps = pl.num_programs(1)
    n_valid = n_valid_ref[b]
    ts = MOE_TILE
    unroll = SUBLANES
    tile8 = lambda r: pl.ds(pl.multiple_of(r * TOKEN_TILES, TOKEN_TILES), TOKEN_TILES)
    rows8 = lambda start: pl.ds(pl.multiple_of(start, TOKEN_TILES), TOKEN_TILES)

    def gather_row(tile, buf, r):
        g_ref[buf, tile8(r), :] = xt_ref[rows8(src_ref[tile * ts + r]), :]

    def scatter_row(tile, buf, r, live=1.0):
        dst = rows8(src_ref[tile * ts + r])
        out_ref[dst, :] = out_ref[dst, :] + (ws_ref[tile * ts + r] * live) * y_ref[buf, tile8(r), :]

    def row_loop(row_fn, tile, buf):
        def group(g, carry):
            for u in range(unroll):
                row_fn(tile, buf, g * unroll + u)
            return carry
        lax.fori_loop(0, ts // unroll, group, 0)

    @pl.when((b == 0) & (j == 0))
    def _():
        g_ref[...] = jnp.zeros_like(g_ref)
        y_ref[...] = jnp.zeros_like(y_ref)

    @pl.when(j == 0)
    def _():
        out_ref[...] = jnp.zeros_like(out_ref)

        def fill(t, carry):
            d = dest_ref[b * tb + t]
            d1 = d & 0xFFFF
            d2 = d >> 16
            src_ref[d1] = t * TOKEN_TILES
            src_ref[d2] = t * TOKEN_TILES
            ws_ref[d1] = w1_ref[b * tb + t]
            ws_ref[d2] = w2_ref[b * tb + t]
            return carry

        lax.fori_loop(0, tb, fill, 0, unroll=unroll)

        def pad_tile(tile, carry):
            n_rows = tile_rows_ref[b * n_steps + tile]
            last = tile * ts + n_rows - 1

            def pad_row(r, c):
                src_ref[tile * ts + r] = src_ref[last]
                ws_ref[tile * ts + r] = 0.0
                return c

            return lax.fori_loop(n_rows, ts, pad_row, carry)

        lax.fori_loop(0, n_valid, pad_tile, 0)
        row_loop(gather_row, 0, 0)

    @pl.when(j < n_valid)
    def _():
        cur = j & 1
        nxt = 1 - cur
        tile_next = jnp.minimum(j + 1, n_valid - 1)
        tile_prev = jnp.maximum(j - 1, 0)
        x = _from_token_layout(g_ref.at[cur], ts).astype(BF16)
        n_chunks = len(MOE_FF_CHUNKS)
        live = jnp.where(j > 0, 1.0, 0.0)

        def move_rows(c):
            for r in range(c * ts // n_chunks, (c + 1) * ts // n_chunks):
                gather_row(tile_next, nxt, r)
                scatter_row(tile_prev, nxt, r, live)

        _to_token_layout(y_ref.at[cur], _swiglu(x, wg_ref, wu_ref, wd_ref, move_rows))

    @pl.when(j == n_valid)
    def _():
        row_loop(scatter_row, n_valid - 1, (n_valid - 1) & 1)


def _moe_plan(meta, cnt, B, L, cb):
    ts = MOE_TILE
    n_steps = TOP_K * L // ts + N_EXPERTS + 1
    m = meta.reshape(B, L // cb, SUBLANES, cb).transpose(2, 0, 1, 3).reshape(SUBLANES, B, L)
    e1, e2, r1, r2 = m[0], m[1], m[2], m[3]
    w1 = lax.bitcast_convert_type(m[4], F32).reshape(B * L)
    w2 = lax.bitcast_convert_type(m[5], F32).reshape(B * L)
    cnt = cnt[:, :, 0]
    padded = (cnt + ts - 1) // ts * ts
    ends = jnp.cumsum(padded, axis=1)
    off = ends - padded
    experts = jnp.arange(N_EXPERTS, dtype=I32)

    def pick(table, idx):
        return jnp.sum(jnp.where(idx[:, :, None] == experts, table[:, None, :], 0), axis=2)

    d1 = pick(off, e1) + r1
    d2 = pick(off, e2) + r2
    dest = (d1 | (d2 << 16)).reshape(B * L).astype(I32)
    start = jnp.arange(n_steps, dtype=I32) * ts
    tile_e = jnp.sum((ends[:, None, :] <= start[None, :, None]).astype(I32), axis=2)
    valid = start[None, :] < ends[:, -1:]
    last_e = jnp.max(jnp.where(valid, tile_e, 0), axis=1, keepdims=True)
    tile_e = jnp.where(valid, tile_e, last_e)
    filled = pick(cnt, tile_e) - (start[None, :] - pick(off, tile_e))
    tile_rows = jnp.where(valid, jnp.clip(filled, 0, ts), 0)
    n_valid = (ends[:, -1] // ts).astype(I32)
    return dest, w1, w2, tile_e.reshape(-1).astype(I32), tile_rows.reshape(-1).astype(I32), n_valid, n_steps


def _moe_sparse(xt, plan, wg, wu, wd, j, B, L):
    dest, w1, w2, tile_e, tile_rows, n_valid, n_steps = plan
    ts = MOE_TILE
    tf = D_FF_TILE
    tok = pl.BlockSpec((L * TOKEN_TILES, LANES), lambda b, k, *_: (b, 0))
    tok_out = pl.BlockSpec((L * TOKEN_TILES, LANES), lambda b, k, *_: (b, 0), pipeline_mode=pl.Buffered(1))
    expert = lambda b, k, dest, w1, w2, te, tr, nv: (j, te[b * n_steps + k], 0, 0)
    grid_spec = pltpu.PrefetchScalarGridSpec(
        num_scalar_prefetch=6,
        grid=(B, n_steps),
        in_specs=[tok,
                  pl.BlockSpec((None, None, D_MODEL, tf), expert),
                  pl.BlockSpec((None, None, D_MODEL, tf), expert),
                  pl.BlockSpec((None, None, tf, D_MODEL), expert)],
        out_specs=tok_out,
        scratch_shapes=[pltpu.SMEM((n_steps * ts,), I32), pltpu.SMEM((n_steps * ts,), F32),
                        pltpu.VMEM((2, ts * TOKEN_TILES, LANES), F32), pltpu.VMEM((2, ts * TOKEN_TILES, LANES), F32)],
    )
    return pl.pallas_call(
        functools.partial(_moe_sparse_kernel, tb=L),
        grid_spec=grid_spec,
        out_shape=jax.ShapeDtypeStruct(xt.shape, F32),
        compiler_params=_params("arbitrary", "arbitrary"),
        name="moe_sparse",
    )(dest, w1, w2, tile_e, tile_rows, n_valid, xt, wg, wu, wd)


def _ple_kernel(*refs, final, add_tokens):
    refs = list(refs)
    h_ref = refs.pop(0)
    m_ref = refs.pop(0) if add_tokens else None
    p_ref, wg_ref, wp_ref = refs[:3]
    fn_ref = refs[3] if final else None
    out_ref = refs[-1]
    h = h_ref[...]
    if add_tokens:
        h = h + _from_token_layout(m_ref, h.shape[0])
    h3 = _ple_update(h, p_ref, wg_ref, wp_ref)
    out_ref[...] = _rms(h3, fn_ref[...]) if final else h3


def _ple(h, m_t, p, W, i, tm):
    T = h.shape[0]
    final = i == DEPTH - 1
    add_tokens = m_t is not None
    rows = pl.BlockSpec((tm, D_MODEL), lambda t: (t, 0))
    in_specs = [rows]
    args = [h]
    if add_tokens:
        in_specs.append(pl.BlockSpec((tm * TOKEN_TILES, LANES), lambda t: (t, 0)))
        args.append(m_t)
    in_specs += [pl.BlockSpec((None, tm, PLE_DIM), lambda t: (i, t, 0)),
                 pl.BlockSpec((None, D_MODEL, D_MODEL), lambda t: (i, 0, 0)),
                 pl.BlockSpec((None, PLE_DIM, D_MODEL), lambda t: (i, 0, 0))]
    args += [p, W["w_ple_gate"], W["w_ple"]]
    if final:
        in_specs.append(pl.BlockSpec((1, D_MODEL), lambda t: (0, 0)))
        args.append(W["final_norm"])
    return pl.pallas_call(
        functools.partial(_ple_kernel, final=final, add_tokens=add_tokens),
        grid=(T // tm,),
        in_specs=in_specs,
        out_specs=rows,
        out_shape=jax.ShapeDtypeStruct((T, D_MODEL), F32),
        compiler_params=_params("arbitrary"),
        name="ple_final" if final else "ple",
    )(*args)


def _rope_tables(pos):
    half = RET_HEAD_DIM // 2
    inv = ROPE_THETA ** (-np.arange(half, dtype=np.float64) / half)
    ang = np.asarray(pos, np.float64)[:, None] * inv[None, :]
    cos, sin = np.cos(ang), np.sin(ang)
    reps = LANES // RET_HEAD_DIM
    return (jnp.asarray(np.tile(np.concatenate([cos, cos], axis=-1), (1, reps)), F32),
            jnp.asarray(np.tile(np.concatenate([-sin, sin], axis=-1), (1, reps)), F32))


def _log_decay():
    return np.log1p(-np.power(2.0, -5.0 - np.arange(RET_HEADS, dtype=np.float64)))


def _retention_tables(C):
    log_g = _log_decay()
    idx = np.arange(C, dtype=np.float64)
    rel = idx[:, None] - idx[None, :]
    dmask = np.where(rel >= 0, np.exp(log_g[:, None, None] * np.maximum(rel, 0.0)), 0.0)
    xi = np.exp(log_g[None, :] * (idx[:, None] + 1.0))
    zeta = np.exp(log_g[None, :] * (C - 1.0 - idx)[:, None])
    spread = lambda t: jnp.asarray(np.repeat(t, RET_HEAD_DIM, axis=1), F32)
    return {"dmask": jnp.asarray(dmask, F32), "xi": spread(xi), "zeta": spread(zeta), "decay_c": _chunk_decay(C)}


def _chunk_decay(C):
    return tuple(math.exp(lg * C) for lg in _log_decay())


def _choose_tile(T, cap):
    tm = min(T, cap)
    while T % tm:
        tm //= 2
    return tm


def _layer_prompt(i, h, p, W, ffn_weights, rope, tabs, B, L):
    T = B * L
    tm = _choose_tile(T, 512)
    cb = _choose_tile(L, 256)
    j = i // 2
    z = _in_proj(h, W["norm1"], W["w_in"], i, rope[0], rope[1], _choose_tile(L, tm))
    if i % 2 == 0:
        assert i < DEPTH - 1
        h1, hn2, pool, ret = _mixer_prompt(z, h, tabs, W, i, B, L, cb)
        return _ffn(hn2, h1, None, *ffn_weights, 0, tm, ple=(p, W, i)), pool, ret
    h1, xt, meta, cnt, pool, ret = _mixer_prompt(z, h, tabs, W, i, B, L, cb)
    m_t = _moe_sparse(xt, _moe_plan(meta, cnt, B, L, cb), *ffn_weights, 0, B, L)
    return _ple(h1, m_t, p, W, i, tm), pool, ret


def _layer_sample(i, h, p, W, rope, decay_1, buf, s0):
    B = h.shape[0]
    j = i // 2
    z = _in_proj(h, W["norm1"], W["w_in"], i, rope[0], rope[1], B)
    h1, hn2, gates, pool, ret = _mixer_sample(z, h, buf, s0, decay_1, W, i, _choose_tile(B, 16))
    if i % 2 == 0:
        h2, *ffn_weights = _ffn(hn2, h1, None, W["ffn_g"], W["ffn_u"], W["ffn_d"], j, B, cast_weights=True)
    else:
        h2, *ffn_weights = _ffn(hn2, h1, gates, W["moe_g"], W["moe_u"], W["moe_d"], j, B, cast_weights=True)
    return _ple(h2, None, p, W, i, B), pool, ret, ffn_weights


def _prepare_weights(norm1, w_in, w_pool, pool_scale, w_o, norm2, ffn_w_gate, ffn_w_up, ffn_w_down,
                     moe_router, moe_w_gate, moe_w_up, moe_w_down, w_ple, w_ple_gate, final_norm):
    r_hi, r_lo = _split_bf16(jnp.pad(moe_router, ((0, 0), (0, 0), (0, LANES - N_EXPERTS))))
    rt_hi, rt_lo = _split_bf16(moe_router.transpose(0, 2, 1))
    return {
        "norm1": norm1[:, None, :], "norm2": norm2[:, None, :], "final_norm": final_norm[None, :],
        "pool_scale": pool_scale[:, None, :],
        "w_in": w_in.astype(BF16), "w_pool": w_pool.astype(BF16), "w_o": w_o.astype(BF16),
        "ffn_g": ffn_w_gate, "ffn_u": ffn_w_up, "ffn_d": ffn_w_down,
        "moe_g": moe_w_gate, "moe_u": moe_w_up, "moe_d": moe_w_down,
        "router_hi": r_hi, "router_lo": r_lo, "router_t_hi": rt_hi, "router_t_lo": rt_lo,
        "w_ple": w_ple.astype(BF16), "w_ple_gate": w_ple_gate.astype(BF16),
    }


def _trunk(x_prompt, x_sample, state_pool, state_ret, p_prompt, p_sample, W):
    B, L, _ = x_prompt.shape
    Bs = x_sample.shape[0]
    hp = x_prompt.reshape(B * L, D_MODEL)
    hs = x_sample.reshape(Bs, D_MODEL)
    pp = p_prompt.reshape(DEPTH, B * L, PLE_DIM)
    ps = p_sample.reshape(DEPTH, Bs, PLE_DIM)
    rope_p = _rope_tables(np.arange(L))
    rope_s = _rope_tables(PAST_LEN + np.arange(1))
    tabs = _retention_tables(_choose_tile(L, 256))
    decay_1 = _chunk_decay(1)
    pool_p, ret_p, pool_s, ret_s = [], [], [], []
    for i in range(DEPTH):
        hs, bs, ss, ffn_weights = _layer_sample(i, hs, ps, W, rope_s, decay_1,
                                                state_pool[i].reshape(Bs, POOL_BUF * POOL_WIDTH), state_ret[i])
        hp, bp, sp = _layer_prompt(i, hp, pp, W, ffn_weights, rope_p, tabs, B, L)
        pool_p.append(bp)
        ret_p.append(sp)
        pool_s.append(bs)
        ret_s.append(ss)
    return (hp.reshape(B, L, D_MODEL), hs.reshape(Bs, 1, D_MODEL), jnp.stack(pool_p), jnp.stack(ret_p),
            jnp.stack(pool_s).reshape(DEPTH, Bs, POOL_BUF, POOL_WIDTH), jnp.stack(ret_s))


def kernel(x_prompt, x_sample, state_pool, state_ret, p_prompt, p_sample, norm1, w_in, w_pool, pool_scale, w_o, norm2, ffn_w_gate, ffn_w_up, ffn_w_down, moe_router, moe_w_gate, moe_w_up, moe_w_down, w_ple, w_ple_gate, final_norm):
    W = _prepare_weights(norm1, w_in, w_pool, pool_scale, w_o, norm2, ffn_w_gate, ffn_w_up, ffn_w_down,
                         moe_router, moe_w_gate, moe_w_up, moe_w_down, w_ple, w_ple_gate, final_norm)
    return _trunk(x_prompt, x_sample, state_pool, state_ret, p_prompt, p_sample, W)
```

```python
import functools
import math

import numpy as np

import jax
import jax.numpy as jnp
from jax import lax
from jax.experimental import pallas as pl
from jax.experimental.pallas import tpu as pltpu

F32 = jnp.float32
BF16 = jnp.bfloat16
I32 = jnp.int32

D_MODEL = 1024
DEPTH = 4
PAST_LEN = 16384
POOL_WIDTH = 512
POOL_WINDOWS = (2, 4, 8, 16)
POOL_GROUP_DIM = 128
POOL_BUF = 15
RET_HEADS = 8
RET_HEAD_DIM = 64
RET_WIDTH = 512
ROPE_THETA = 10000.0
IN_WIDTH = POOL_WIDTH + 4 * RET_WIDTH
SECTION = 512
D_FF_TILE = 1408
N_EXPERTS = 8
TOP_K = 2
PLE_DIM = 256
NORM_EPS = 1e-6
LANES = 128
SUBLANES = 8
TOKEN_TILES = D_MODEL // LANES
assert TOKEN_TILES == SUBLANES
TAIL_ROWS = 16
MOE_TILE = 256
MOE_FF_CHUNKS = ((0, 512), (512, 1024), (1024, D_FF_TILE))
VMEM_LIMIT = 58 * 1024 * 1024


def _rms(x, g):
    return x * lax.rsqrt(jnp.mean(x * x, axis=-1, keepdims=True) + NORM_EPS) * g


def _dot(a, b):
    return jnp.dot(a, b, preferred_element_type=F32)


def _dot_nt(a, b):
    return lax.dot_general(a, b, (((1,), (1,)), ((), ())), preferred_element_type=F32)


def _dot_tn(a, b):
    return lax.dot_general(a, b, (((0,), (0,)), ((), ())), preferred_element_type=F32)


def _params(*sem):
    return pltpu.CompilerParams(dimension_semantics=sem, vmem_limit_bytes=VMEM_LIMIT)


def _split_bf16(x):
    hi = x.astype(BF16)
    return hi, (x - hi.astype(F32)).astype(BF16)


def _to_token_layout(ref, x):
    n = x.shape[0]
    for j in range(TOKEN_TILES):
        ref[pl.ds(j, n, stride=TOKEN_TILES), :] = x[:, j * LANES:(j + 1) * LANES]


def _from_token_layout(ref, n):
    return jnp.concatenate([ref[pl.ds(j, n, stride=TOKEN_TILES), :] for j in range(TOKEN_TILES)], axis=1)


def _rope_slab(x, cos, sin, first_half):
    fwd = pltpu.roll(x, 32, 1)
    bwd = pltpu.roll(x, LANES - 32, 1)
    return x * cos + jnp.where(first_half, bwd, fwd) * sin


def _in_proj_kernel(h_ref, n1_ref, w_ref, cos_ref, sin_ref, z_ref):
    hn = _rms(h_ref[...], n1_ref[...]).astype(BF16)
    cos = cos_ref[...]
    sin = sin_ref[...]
    lane = lax.broadcasted_iota(I32, (hn.shape[0], LANES), 1)
    first_half = (lane & 32) == 0
    for s in range(IN_WIDTH // SECTION):
        zs = _dot(hn, w_ref[:, s * SECTION:(s + 1) * SECTION])
        if s in (1, 2):
            scale = 1.0 if s == 1 else RET_HEAD_DIM ** -0.5
            for c in range(SECTION // LANES):
                slab = _rope_slab(zs[:, c * LANES:(c + 1) * LANES], cos, sin, first_half)
                z_ref[:, s * SECTION + c * LANES:s * SECTION + (c + 1) * LANES] = slab * scale
        else:
            z_ref[:, s * SECTION:(s + 1) * SECTION] = zs


def _in_proj(h, n1, w_in, i, cos_t, sin_t, tm):
    T = h.shape[0]
    n_tab = cos_t.shape[0] // tm if cos_t.shape[0] > 1 else 1
    tb = tm if cos_t.shape[0] > 1 else 1
    return pl.pallas_call(
        _in_proj_kernel,
        grid=(T // tm,),
        in_specs=[
            pl.BlockSpec((tm, D_MODEL), lambda t: (t, 0)),
            pl.BlockSpec((None, 1, D_MODEL), lambda t: (i, 0, 0)),
            pl.BlockSpec((None, D_MODEL, IN_WIDTH), lambda t: (i, 0, 0)),
            pl.BlockSpec((tb, LANES), lambda t: (t % n_tab, 0)),
            pl.BlockSpec((tb, LANES), lambda t: (t % n_tab, 0)),
        ],
        out_specs=pl.BlockSpec((tm, IN_WIDTH), lambda t: (t, 0)),
        out_shape=jax.ShapeDtypeStruct((T, IN_WIDTH), F32),
        compiler_params=_params("arbitrary"),
        name="in_proj",
    )(h, n1, w_in, cos_t, sin_t)


def _top2(lg, idx, axis, n_idx):
    neg = jnp.float32(-jnp.inf)
    m1 = jnp.max(lg, axis=axis, keepdims=True)
    i1 = jnp.min(jnp.where(lg == m1, idx, n_idx), axis=axis, keepdims=True)
    lg2 = jnp.where(idx == i1, neg, lg)
    m2 = jnp.max(lg2, axis=axis, keepdims=True)
    i2 = jnp.min(jnp.where(lg2 == m2, idx, n_idx), axis=axis, keepdims=True)
    e2 = jnp.exp(m2 - m1)
    den = 1.0 + e2
    return i1, i2, 1.0 / den, e2 / den


def _route(x, rhi_ref, rlo_ref):
    x_hi, x_lo = _split_bf16(x)
    logits = _dot(x_hi, rhi_ref[...]) + _dot(x_lo, rhi_ref[...]) + _dot(x_hi, rlo_ref[...])
    lane = lax.broadcasted_iota(I32, logits.shape, 1)
    lg = jnp.where(lane < N_EXPERTS, logits, jnp.float32(-jnp.inf))
    i1, i2, w1, w2 = _top2(lg, lane, 1, LANES)
    return jnp.where(lane == i1, w1, 0.0) + jnp.where(lane == i2, w2, 0.0)


def _route_logits_t(x, rthi_ref, rtlo_ref):
    x_hi, x_lo = _split_bf16(x)
    return _dot_nt(rthi_ref[...], x_hi) + _dot_nt(rthi_ref[...], x_lo) + _dot_nt(rtlo_ref[...], x_hi)


def _route_ranks(lt, fresh, live, carry_ref, meta_ref, cnt_ref):
    n = lt.shape[1]
    sub = lax.broadcasted_iota(I32, lt.shape, 0)
    i1, i2, w1, w2 = _top2(lt, sub, 0, N_EXPERTS)
    ind = jnp.where(sub == i1, 1.0, 0.0) + jnp.where(sub == i2, 1.0, 0.0)
    tri = jnp.where(lax.broadcasted_iota(I32, (n, n), 0) <= lax.broadcasted_iota(I32, (n, n), 1), 1.0, 0.0)
    csum = _dot(ind.astype(BF16), tri.astype(BF16))
    carry = carry_ref[...] * (1.0 - fresh)
    rank = carry[:, 0:1] + csum - ind
    r1 = jnp.sum(jnp.where(sub == i1, rank, 0.0), axis=0, keepdims=True).astype(I32)
    r2 = jnp.sum(jnp.where(sub == i2, rank, 0.0), axis=0, keepdims=True).astype(I32)
    carry_ref[...] = carry + csum[:, n - 1:n] * live
    cnt_ref[0] = carry_ref[...].astype(I32)
    rows = [i1, i2, r1, r2, pltpu.bitcast(w1, I32), pltpu.bitcast(w2, I32)]
    meta = jnp.zeros(lt.shape, I32)
    for j, rowv in enumerate(rows):
        meta = jnp.where(sub == j, jnp.broadcast_to(rowv, lt.shape), meta)
    meta_ref[0] = meta


def _pool_project(d_slabs, u, wpool_ref, pscale_ref):
    ys = []
    for g in range(len(POOL_WINDOWS)):
        sl = slice(g * POOL_GROUP_DIM, (g + 1) * POOL_GROUP_DIM)
        ys.append(_dot((d_slabs[g] - u[:, sl]).astype(BF16), wpool_ref[g]) * pscale_ref[:, sl])
    return ys


def _mixer_prompt_kernel(*refs, cb, nc, n_chunks, decay_c, moe):
    (u_ref, q_ref, k_ref, v_ref, g_ref, h_ref, dmask_ref, xi_ref, zeta_ref,
     wpool_ref, pscale_ref, wo_ref, n2_ref) = refs[:13]
    if moe:
        (rthi_ref, rtlo_ref, h1_ref, xt_ref, meta_ref, cnt_ref, pool_out_ref, ret_out_ref,
         s_scr, p_scr, mix_scr, carry_scr) = refs[13:]
    else:
        h1_ref, hn2_ref, pool_out_ref, ret_out_ref, s_scr, p_scr, mix_scr = refs[13:]
    step = pl.program_id(0)
    c = jnp.minimum(step, n_chunks - 1) % nc
    c_tail = jnp.maximum(step - 1, 0) % nc

    @pl.when(step == 0)
    def _():
        mix_scr[...] = jnp.zeros_like(mix_scr)
        if moe:
            carry_scr[...] = jnp.zeros_like(carry_scr)

    @pl.when(c == 0)
    def _():
        s_scr[...] = jnp.zeros_like(s_scr)
        p_scr[0:TAIL_ROWS, :] = jnp.zeros((TAIL_ROWS, POOL_WIDTH), F32)

    proj = _dot(mix_scr[...], wo_ref[...])

    u = u_ref[...]
    old_tail = p_scr[0:TAIL_ROWS, :]
    p_scr[TAIL_ROWS:TAIL_ROWS + cb, :] = u
    pos = c * cb + lax.broadcasted_iota(I32, (cb, POOL_GROUP_DIM), 0)
    means = []
    for g, w in enumerate(POOL_WINDOWS):
        sl = slice(g * POOL_GROUP_DIM, (g + 1) * POOL_GROUP_DIM)
        s = u[:, sl]
        for j in range(1, w):
            s = s + p_scr[TAIL_ROWS - j:TAIL_ROWS - j + cb, sl]
        means.append(s / jnp.minimum(pos + 1, w).astype(F32))
    ys = _pool_project(means, u, wpool_ref, pscale_ref)
    new_tail = p_scr[cb:cb + TAIL_ROWS, :]

    h1 = h_ref[...] + proj
    h1_ref[...] = h1
    hn2 = _rms(h1, n2_ref[...])

    q = q_ref[...]
    k = k_ref[...]
    qb = q.astype(BF16)
    kb = k.astype(BF16)
    kz = (k * zeta_ref[...]).astype(BF16)
    vb = v_ref[...].astype(BF16)
    xi = xi_ref[...]
    heads = range(RET_HEADS)
    sls = [slice(hd * RET_HEAD_DIM, (hd + 1) * RET_HEAD_DIM) for hd in heads]
    s_prev = [s_scr[hd] for hd in heads]
    sc = [_dot_nt(qb[:, sl], kb[:, sl]) for sl in sls]
    cross = [_dot(qb[:, sl], s_prev[hd].astype(BF16)) for hd, sl in enumerate(sls)]
    upd = [_dot_tn(kz[:, sl], vb[:, sl]) for sl in sls]

    if moe:
        _to_token_layout(xt_ref, hn2)
        lt = _route_logits_t(hn2, rthi_ref, rtlo_ref)
    else:
        hn2_ref[...] = hn2.astype(BF16)

    pm = [(sc[hd] * dmask_ref[hd]).astype(BF16) for hd in heads]
    o_h = [_dot(pm[hd], vb[:, sl]) + cross[hd] * xi[:, sl] for hd, sl in enumerate(sls)]
    new_s = [s_prev[hd] * decay_c[hd] + upd[hd] for hd in heads]

    if moe:
        _route_ranks(lt, jnp.where(c_tail == 0, 1.0, 0.0), jnp.where(step > 0, 1.0, 0.0), carry_scr, meta_ref, cnt_ref)

    mu = [jnp.mean(o, axis=-1, keepdims=True) for o in o_h]
    cen = [o - m for o, m in zip(o_h, mu)]
    var = [jnp.mean(jnp.square(x), axis=-1, keepdims=True) for x in cen]
    os_ = [x * lax.rsqrt(v + NORM_EPS) for x, v in zip(cen, var)]
    o = jnp.concatenate(os_, axis=1) * jax.nn.silu(g_ref[...])
    mix_scr[...] = jnp.concatenate(ys + [o], axis=1).astype(BF16)

    real = step < n_chunks
    new_tail = jnp.where(real, new_tail, old_tail)
    p_scr[0:TAIL_ROWS, :] = new_tail
    pool_out_ref[0] = new_tail[TAIL_ROWS - POOL_BUF:, :]
    for hd in heads:
        kept = jnp.where(real, new_s[hd], s_prev[hd])
        s_scr[hd] = kept
        ret_out_ref[0, hd] = kept


def _mixer_prompt(z, h, tabs, W, i, B, L, cb):
    T = B * L
    nc = L // cb
    moe = i % 2 == 1
    n_chunks = B * nc
    head = lambda g: jnp.minimum(g, n_chunks - 1)
    tail = lambda g: jnp.maximum(g - 1, 0)
    zspec = lambda s: pl.BlockSpec((cb, SECTION), lambda g, s=s: (head(g), s))
    const2 = lambda shape: pl.BlockSpec(shape, lambda g: (0, 0))
    const3 = lambda shape: pl.BlockSpec(shape, lambda g: (0, 0, 0))
    layer3 = lambda shape: pl.BlockSpec((None,) + shape, lambda g: (i, 0, 0))
    rows = pl.BlockSpec((cb, D_MODEL), lambda g: (tail(g), 0))
    in_specs = [zspec(0), zspec(1), zspec(2), zspec(3), zspec(4), rows,
                const3((RET_HEADS, cb, cb)), const2((cb, RET_WIDTH)), const2((cb, RET_WIDTH)),
                pl.BlockSpec((None, len(POOL_WINDOWS), POOL_GROUP_DIM, POOL_GROUP_DIM), lambda g: (i, 0, 0, 0)),
                layer3((1, POOL_WIDTH)), layer3((D_MODEL, D_MODEL)), layer3((1, D_MODEL))]
    args = [z, z, z, z, z, h, tabs["dmask"], tabs["xi"], tabs["zeta"],
            W["w_pool"], W["pool_scale"], W["w_o"], W["norm2"]]
    scratch = [pltpu.VMEM((RET_HEADS, RET_HEAD_DIM, RET_HEAD_DIM), F32),
               pltpu.VMEM((TAIL_ROWS + cb, POOL_WIDTH), F32),
               pltpu.VMEM((cb, D_MODEL), BF16)]
    state_specs = [pl.BlockSpec((1, POOL_BUF, POOL_WIDTH), lambda g: (head(g) // nc, 0, 0)),
                   pl.BlockSpec((1, RET_HEADS, RET_HEAD_DIM, RET_HEAD_DIM), lambda g: (head(g) // nc, 0, 0, 0))]
    state_shapes = [jax.ShapeDtypeStruct((B, POOL_BUF, POOL_WIDTH), F32),
                    jax.ShapeDtypeStruct((B, RET_HEADS, RET_HEAD_DIM, RET_HEAD_DIM), F32)]
    if moe:
        j = i // 2
        rt = pl.BlockSpec((None, N_EXPERTS, D_MODEL), lambda g: (j, 0, 0))
        in_specs += [rt, rt]
        args += [W["router_t_hi"], W["router_t_lo"]]
        out_specs = [rows,
                     pl.BlockSpec((cb * TOKEN_TILES, LANES), lambda g: (tail(g), 0)),
                     pl.BlockSpec((1, SUBLANES, cb), lambda g: (tail(g), 0, 0)),
                     pl.BlockSpec((1, N_EXPERTS, LANES), lambda g: (tail(g) // nc, 0, 0))]
        out_shape = [jax.ShapeDtypeStruct((T, D_MODEL), F32),
                     jax.ShapeDtypeStruct((T * TOKEN_TILES, LANES), F32),
                     jax.ShapeDtypeStruct((B * nc, SUBLANES, cb), I32),
                     jax.ShapeDtypeStruct((B, N_EXPERTS, LANES), I32)]
        scratch.append(pltpu.VMEM((N_EXPERTS, LANES), F32))
    else:
        out_specs = [rows, rows]
        out_shape = [jax.ShapeDtypeStruct((T, D_MODEL), F32), jax.ShapeDtypeStruct((T, D_MODEL), BF16)]
    return pl.pallas_call(
        functools.partial(_mixer_prompt_kernel, cb=cb, nc=nc, n_chunks=n_chunks, decay_c=tabs["decay_c"], moe=moe),
        grid=(n_chunks + 1,),
        in_specs=in_specs,
        out_specs=out_specs + state_specs,
        out_shape=out_shape + state_shapes,
        scratch_shapes=scratch,
        compiler_params=_params("arbitrary"),
        name="mixer_prompt",
    )(*args)


def _to_head_rows(ref, x):
    n = x.shape[0]
    for hd in range(RET_HEADS):
        ref[pl.ds(hd, n, stride=RET_HEADS), 0:RET_HEAD_DIM] = x[:, hd * RET_HEAD_DIM:(hd + 1) * RET_HEAD_DIM]


def _mixer_sample_kernel(*refs, bb, decay_1, moe):
    (u_ref, q_ref, k_ref, v_ref, g_ref, h_ref, buf_ref, s_ref,
     wpool_ref, pscale_ref, wo_ref, n2_ref) = refs[:12]
    n_in = 14 if moe else 12
    router_refs = refs[12:n_in]
    h1_ref, hn2_ref = refs[n_in:n_in + 2]
    gates_ref = refs[n_in + 2] if moe else None
    buf_out_ref, s_out_ref, q2_scr, k2_scr, v2_scr, g2_scr, c2_scr, o2_scr, qrep_scr, krep_scr = refs[-10:]
    dh = RET_HEAD_DIM
    grp = SUBLANES

    u = u_ref[...]
    means = []
    for g, w in enumerate(POOL_WINDOWS):
        s = u[:, g * POOL_GROUP_DIM:(g + 1) * POOL_GROUP_DIM]
        for j in range(1, w):
            lo = (POOL_BUF - j) * POOL_WIDTH + g * POOL_GROUP_DIM
            s = s + buf_ref[:, lo:lo + POOL_GROUP_DIM]
        means.append(s / float(min(PAST_LEN + 1, w)))
    ys = _pool_project(means, u, wpool_ref, pscale_ref)
    keep = (POOL_BUF - 1) * POOL_WIDTH
    buf_out_ref[:, 0:keep] = buf_ref[:, POOL_WIDTH:POOL_WIDTH + keep]
    buf_out_ref[:, keep:keep + POOL_WIDTH] = u

    for ref, src in ((q2_scr, q_ref), (k2_scr, k_ref), (v2_scr, v_ref), (g2_scr, g_ref)):
        _to_head_rows(ref, src[...])
    spread = jnp.where(lax.broadcasted_iota(I32, (grp, grp * LANES), 0)
                       == lax.broadcasted_iota(I32, (grp, grp * LANES), 1) // LANES, 1.0, 0.0)

    def per_group(gi, carry):
        rows = pl.ds(pl.multiple_of(gi * grp, grp), grp)
        qrep_scr[...] = _dot_tn(q_ref[rows, :], spread)
        krep_scr[...] = _dot_tn(k_ref[rows, :], spread)
        for bl in range(grp):
            b = gi * grp + bl
            cross = []
            for hd in range(RET_HEADS):
                tile = (slice(hd * dh, (hd + 1) * dh), slice(bl * LANES, bl * LANES + dh))
                s_prev = s_ref[b, hd]
                cross.append(jnp.sum(s_prev * qrep_scr[tile], axis=0, keepdims=True))
                v_row = v2_scr[pl.ds(b * RET_HEADS + hd, 1), 0:dh]
                s_out_ref[b, hd] = s_prev * decay_1[hd] + krep_scr[tile] * v_row
            c2_scr[pl.ds(pl.multiple_of(b * RET_HEADS, RET_HEADS), RET_HEADS), 0:dh] = jnp.concatenate(cross, axis=0)
        return carry

    lax.fori_loop(0, bb // grp, per_group, 0)

    n2 = bb * RET_HEADS
    q2, k2, v2 = q2_scr[:, 0:dh], k2_scr[:, 0:dh], v2_scr[:, 0:dh]
    head = lax.broadcasted_iota(I32, (n2, dh), 0) % RET_HEADS
    gamma = jnp.zeros((n2, dh), F32)
    for hd in range(RET_HEADS):
        gamma = jnp.where(head == hd, decay_1[hd], gamma)
    o2 = jnp.sum(q2 * k2, axis=-1, keepdims=True) * v2 + gamma * c2_scr[:, 0:dh]
    mu = jnp.mean(o2, axis=-1, keepdims=True)
    var = jnp.mean(jnp.square(o2 - mu), axis=-1, keepdims=True)
    o2_scr[:, 0:dh] = (o2 - mu) * lax.rsqrt(var + NORM_EPS) * jax.nn.silu(g2_scr[:, 0:dh])
    o = [o2_scr[pl.ds(hd, bb, stride=RET_HEADS), 0:dh] for hd in range(RET_HEADS)]
    mix = jnp.concatenate(ys + o, axis=1)
    h1 = h_ref[...] + _dot(mix.astype(BF16), wo_ref[...])
    h1_ref[...] = h1
    hn2 = _rms(h1, n2_ref[...])
    hn2_ref[...] = hn2.astype(BF16)
    if moe:
        gates_ref[...] = _route(hn2, *router_refs)


def _mixer_sample(z, h, buf, s0, decay_1, W, i, bb):
    B = h.shape[0]
    moe = i % 2 == 1
    flat = POOL_BUF * POOL_WIDTH
    zspec = lambda s: pl.BlockSpec((bb, SECTION), lambda t, s=s: (t, s))
    layer3 = lambda shape: pl.BlockSpec((None,) + shape, lambda t: (i, 0, 0))
    rows = pl.BlockSpec((bb, D_MODEL), lambda t: (t, 0))
    pool = pl.BlockSpec((bb, flat), lambda t: (t, 0))
    state = pl.BlockSpec((bb, RET_HEADS, RET_HEAD_DIM, RET_HEAD_DIM), lambda t: (t, 0, 0, 0))
    in_specs = [zspec(0), zspec(1), zspec(2), zspec(3), zspec(4), rows, pool, state,
                pl.BlockSpec((None, len(POOL_WINDOWS), POOL_GROUP_DIM, POOL_GROUP_DIM), lambda t: (i, 0, 0, 0)),
                layer3((1, POOL_WIDTH)), layer3((D_MODEL, D_MODEL)), layer3((1, D_MODEL))]
    args = [z, z, z, z, z, h, buf, s0, W["w_pool"], W["pool_scale"], W["w_o"], W["norm2"]]
    out_specs = [rows, rows]
    out_shape = [jax.ShapeDtypeStruct((B, D_MODEL), F32), jax.ShapeDtypeStruct((B, D_MODEL), BF16)]
    if moe:
        j = i // 2
        r = pl.BlockSpec((None, D_MODEL, LANES), lambda t: (j, 0, 0))
        in_specs += [r, r]
        args += [W["router_hi"], W["router_lo"]]
        out_specs.append(pl.BlockSpec((bb, LANES), lambda t: (t, 0)))
        out_shape.append(jax.ShapeDtypeStruct((B, LANES), F32))
    out_specs += [pool, state]
    out_shape += [jax.ShapeDtypeStruct((B, flat), F32), jax.ShapeDtypeStruct(s0.shape, F32)]
    outs = pl.pallas_call(
        functools.partial(_mixer_sample_kernel, bb=bb, decay_1=decay_1, moe=moe),
        grid=(B // bb,),
        in_specs=in_specs,
        out_specs=out_specs,
        out_shape=out_shape,
        scratch_shapes=[pltpu.VMEM((bb * RET_HEADS, LANES), F32) for _ in range(6)]
        + [pltpu.VMEM((RET_WIDTH, SUBLANES * LANES), F32) for _ in range(2)],
        compiler_params=_params("arbitrary"),
        name="mixer_sample",
    )(*args)
    if moe:
        h1, hn2, gates, pool, ret = outs
    else:
        (h1, hn2, pool, ret), gates = outs, None
    return h1, hn2, gates, pool, ret


def _swiglu(x, wg_ref, wu_ref, wd_ref, between_chunks=None):
    y = None
    for c, (lo, hi) in enumerate(MOE_FF_CHUNKS):
        if between_chunks is not None:
            between_chunks(c)
        hidden = jax.nn.silu(_dot(x, wg_ref[:, lo:hi])) * _dot(x, wu_ref[:, lo:hi])
        part = _dot(hidden.astype(BF16), wd_ref[lo:hi, :])
        y = part if y is None else y + part
    return y


def _ple_update(h, p_ref, wg_ref, wp_ref):
    gate = jax.nn.sigmoid(_dot(h.astype(BF16), wg_ref[...]))
    return h + gate * _dot(p_ref[...].astype(BF16), wp_ref[...])


def _ffn_kernel(*refs, moe, ple, cast_weights):
    refs = list(refs)
    x_ref, h_ref = refs[:2]
    del refs[:2]
    gates_ref = refs.pop(0) if moe else None
    wg_ref, wu_ref, wd_ref = refs[:3]
    ple_refs = refs[3:6] if ple else None
    acc_ref = refs[-1]
    if cast_weights:
        out_ref, wg_out, wu_out, wd_out = refs[-5:-1]
        for src, dst in ((wg_ref, wg_out), (wu_ref, wu_out), (wd_ref, wd_out)):
            dst[...] = src[...].astype(BF16)
        wg_ref, wu_ref, wd_ref = wg_out, wu_out, wd_out
    else:
        out_ref = refs[-2]
    f = pl.program_id(1)
    y = _swiglu(x_ref[...], wg_ref, wu_ref, wd_ref)
    if moe:
        gates = gates_ref[...]
        lane = lax.broadcasted_iota(I32, gates.shape, 1)
        y = y * jnp.sum(jnp.where(lane == f, gates, 0.0), axis=-1, keepdims=True)

    @pl.when(f == 0)
    def _():
        acc_ref[...] = y

    @pl.when(f > 0)
    def _():
        acc_ref[...] += y

    @pl.when(f == pl.num_programs(1) - 1)
    def _():
        h2 = h_ref[...] + acc_ref[...]
        out_ref[...] = _ple_update(h2, *ple_refs) if ple else h2


def _ffn(x, h, gates, wg, wu, wd, j, tm, ple=None, cast_weights=False):
    T = x.shape[0]
    moe = gates is not None
    tf = D_FF_TILE
    row = pl.BlockSpec((tm, D_MODEL), lambda t, f: (t, 0))
    in_specs = [row, row]
    args = [x, h]
    if moe:
        nf = wg.shape[1]
        in_specs.append(pl.BlockSpec((tm, LANES), lambda t, f: (t, 0)))
        args.append(gates)
        w_blocks = [((None, None, D_MODEL, tf), lambda jj: lambda t, f: (jj, f, 0, 0))] * 2
        w_blocks.append(((None, None, tf, D_MODEL), lambda jj: lambda t, f: (jj, f, 0, 0)))
    else:
        nf = wg.shape[2] // tf
        w_blocks = [((None, D_MODEL, tf), lambda jj: lambda t, f: (jj, 0, f))] * 2
        w_blocks.append(((None, tf, D_MODEL), lambda jj: lambda t, f: (jj, f, 0)))
    in_specs += [pl.BlockSpec(shape, index(j)) for shape, index in w_blocks]
    args += [wg, wu, wd]
    if ple is not None:
        p, W, i = ple
        in_specs += [pl.BlockSpec((None, tm, PLE_DIM), lambda t, f: (i, t, 0)),
                     pl.BlockSpec((None, D_MODEL, D_MODEL), lambda t, f: (i, 0, 0)),
                     pl.BlockSpec((None, PLE_DIM, D_MODEL), lambda t, f: (i, 0, 0))]
        args += [p, W["w_ple_gate"], W["w_ple"]]
    out_specs = row
    out_shape = jax.ShapeDtypeStruct((T, D_MODEL), F32)
    if cast_weights:
        assert T == tm, "each weight block must be visited exactly once"
        out_specs = [row] + [pl.BlockSpec(shape, index(0)) for shape, index in w_blocks]
        out_shape = [out_shape] + [jax.ShapeDtypeStruct((1,) + w.shape[1:], BF16) for w in (wg, wu, wd)]
    return pl.pallas_call(
        functools.partial(_ffn_kernel, moe=moe, ple=ple is not None, cast_weights=cast_weights),
        grid=(T // tm, nf),
        in_specs=in_specs,
        out_specs=out_specs,
        out_shape=out_shape,
        scratch_shapes=[pltpu.VMEM((tm, D_MODEL), F32)],
        compiler_params=_params("arbitrary", "arbitrary"),
        name="ffn_moe" if moe else "ffn_dense",
    )(*args)


def _moe_sparse_kernel(dest_ref, w1_ref, w2_ref, tile_e_ref, tile_rows_ref, n_valid_ref,
                       xt_ref, wg_ref, wu_ref, wd_ref, out_ref,
                       src_ref, ws_ref, g_ref, y_ref, *, tb):
    b = pl.program_id(0)
    j = pl.program_id(1)
    n_steps = pl.num_programs(1)
    n_valid = n_valid_ref[b]
    ts = MOE_TILE
    unroll = SUBLANES
    tile8 = lambda r: pl.ds(pl.multiple_of(r * TOKEN_TILES, TOKEN_TILES), TOKEN_TILES)
    rows8 = lambda start: pl.ds(pl.multiple_of(start, TOKEN_TILES), TOKEN_TILES)
    slot_rows = tb * TOKEN_TILES

    def gather_row(tile, buf, r):
        g_ref[buf, tile8(r), :] = xt_ref[rows8(src_ref[tile * ts + r] & (slot_rows - 1)), :]

    def scatter_row(tile, buf, r):
        out_ref[rows8(src_ref[tile * ts + r]), :] = ws_ref[tile * ts + r] * y_ref[buf, tile8(r), :]

    def row_loop(row_fn, tile, buf):
        def group(g, carry):
            for u in range(unroll):
                row_fn(tile, buf, g * unroll + u)
            return carry
        lax.fori_loop(0, ts // unroll, group, 0)

    @pl.when((b == 0) & (j == 0))
    def _():
        g_ref[...] = jnp.zeros_like(g_ref)
        y_ref[...] = jnp.zeros_like(y_ref)

    @pl.when(j == 0)
    def _():
        def fill(t, carry):
            d = dest_ref[b * tb + t]
            d1 = d & 0xFFFF
            d2 = d >> 16
            src_ref[d1] = t * TOKEN_TILES
            src_ref[d2] = t * TOKEN_TILES + slot_rows
            ws_ref[d1] = w1_ref[b * tb + t]
            ws_ref[d2] = w2_ref[b * tb + t]
            return carry

        lax.fori_loop(0, tb, fill, 0, unroll=unroll)

        def pad_tile(tile, carry):
            n_rows = tile_rows_ref[b * n_steps + tile]
            last = tile * ts + n_rows - 1

            def pad_row(r, c):
                src_ref[tile * ts + r] = src_ref[last]
                ws_ref[tile * ts + r] = ws_ref[last]
                return c

            return lax.fori_loop(n_rows, ts, pad_row, carry)

        lax.fori_loop(0, n_valid, pad_tile, 0)
        row_loop(gather_row, 0, 0)

    @pl.when(j < n_valid)
    def _():
        cur = j & 1
        nxt = 1 - cur
        tile_next = jnp.minimum(j + 1, n_valid - 1)
        tile_prev = jnp.maximum(j - 1, 0)
        x = _from_token_layout(g_ref.at[cur], ts).astype(BF16)
        n_chunks = len(MOE_FF_CHUNKS)

        def move_rows(c):
            for r in range(c * ts // n_chunks, (c + 1) * ts // n_chunks):
                gather_row(tile_next, nxt, r)
                scatter_row(tile_prev, nxt, r)

        _to_token_layout(y_ref.at[cur], _swiglu(x, wg_ref, wu_ref, wd_ref, move_rows))

    @pl.when(j == n_valid)
    def _():
        row_loop(scatter_row, n_valid - 1, (n_valid - 1) & 1)


def _moe_plan(meta, cnt, B, L, cb):
    ts = MOE_TILE
    n_steps = TOP_K * L // ts + N_EXPERTS + 1
    m = meta.reshape(B, L // cb, SUBLANES, cb).transpose(2, 0, 1, 3).reshape(SUBLANES, B, L)
    e1, e2, r1, r2 = m[0], m[1], m[2], m[3]
    w1 = lax.bitcast_convert_type(m[4], F32).reshape(B * L)
    w2 = lax.bitcast_convert_type(m[5], F32).reshape(B * L)
    cnt = cnt[:, :, 0]
    padded = (cnt + ts - 1) // ts * ts
    ends = jnp.cumsum(padded, axis=1)
    off = ends - padded
    experts = jnp.arange(N_EXPERTS, dtype=I32)

    def pick(table, idx):
        return jnp.sum(jnp.where(idx[:, :, None] == experts, table[:, None, :], 0), axis=2)

    d1 = pick(off, e1) + r1
    d2 = pick(off, e2) + r2
    dest = (d1 | (d2 << 16)).reshape(B * L).astype(I32)
    start = jnp.arange(n_steps, dtype=I32) * ts
    tile_e = jnp.sum((ends[:, None, :] <= start[None, :, None]).astype(I32), axis=2)
    valid = start[None, :] < ends[:, -1:]
    last_e = jnp.max(jnp.where(valid, tile_e, 0), axis=1, keepdims=True)
    tile_e = jnp.where(valid, tile_e, last_e)
    filled = pick(cnt, tile_e) - (start[None, :] - pick(off, tile_e))
    tile_rows = jnp.where(valid, jnp.clip(filled, 0, ts), 0)
    n_valid = (ends[:, -1] // ts).astype(I32)
    return dest, w1, w2, tile_e.reshape(-1).astype(I32), tile_rows.reshape(-1).astype(I32), n_valid, n_steps


def _moe_sparse(xt, plan, wg, wu, wd, j, B, L):
    dest, w1, w2, tile_e, tile_rows, n_valid, n_steps = plan
    ts = MOE_TILE
    tf = D_FF_TILE
    slot_rows = L * TOKEN_TILES
    assert slot_rows & (slot_rows - 1) == 0, "the kernel masks row offsets with slot_rows - 1"
    tok = pl.BlockSpec((slot_rows, LANES), lambda b, k, *_: (b, 0), pipeline_mode=pl.Buffered(1))
    tok_out = pl.BlockSpec((TOP_K * slot_rows, LANES), lambda b, k, *_: (b, 0), pipeline_mode=pl.Buffered(1))
    expert = lambda b, k, dest, w1, w2, te, tr, nv: (j, te[b * n_steps + k], 0, 0)
    grid_spec = pltpu.PrefetchScalarGridSpec(
        num_scalar_prefetch=6,
        grid=(B, n_steps),
        in_specs=[tok,
                  pl.BlockSpec((None, None, D_MODEL, tf), expert),
                  pl.BlockSpec((None, None, D_MODEL, tf), expert),
                  pl.BlockSpec((None, None, tf, D_MODEL), expert)],
        out_specs=tok_out,
        scratch_shapes=[pltpu.SMEM((n_steps * ts,), I32), pltpu.SMEM((n_steps * ts,), F32),
                        pltpu.VMEM((2, ts * TOKEN_TILES, LANES), F32), pltpu.VMEM((2, ts * TOKEN_TILES, LANES), F32)],
    )
    out = pl.pallas_call(
        functools.partial(_moe_sparse_kernel, tb=L),
        grid_spec=grid_spec,
        out_shape=jax.ShapeDtypeStruct((B * TOP_K * slot_rows, LANES), F32),
        compiler_params=_params("arbitrary", "arbitrary"),
        name="moe_sparse",
    )(dest, w1, w2, tile_e, tile_rows, n_valid, xt, wg, wu, wd)
    return out.reshape(B, TOP_K, slot_rows, LANES)


def _ple_kernel(*refs, final, add_tokens):
    refs = list(refs)
    h_ref = refs.pop(0)
    m_ref = refs.pop(0) if add_tokens else None
    p_ref, wg_ref, wp_ref = refs[:3]
    fn_ref = refs[3] if final else None
    out_ref = refs[-1]
    h = h_ref[...]
    if add_tokens:
        for s in range(TOP_K):
            h = h + _from_token_layout(m_ref.at[s], h.shape[0])
    h3 = _ple_update(h, p_ref, wg_ref, wp_ref)
    out_ref[...] = _rms(h3, fn_ref[...]) if final else h3


def _ple(h, m_t, p, W, i, tm):
    T = h.shape[0]
    final = i == DEPTH - 1
    add_tokens = m_t is not None
    rows = pl.BlockSpec((tm, D_MODEL), lambda t: (t, 0))
    in_specs = [rows]
    args = [h]
    if add_tokens:
        per_seq = m_t.shape[2] // (tm * TOKEN_TILES)
        in_specs.append(pl.BlockSpec((None, TOP_K, tm * TOKEN_TILES, LANES),
                                     lambda t: (t // per_seq, 0, t % per_seq, 0)))
        args.append(m_t)
    in_specs += [pl.BlockSpec((None, tm, PLE_DIM), lambda t: (i, t, 0)),
                 pl.BlockSpec((None, D_MODEL, D_MODEL), lambda t: (i, 0, 0)),
                 pl.BlockSpec((None, PLE_DIM, D_MODEL), lambda t: (i, 0, 0))]
    args += [p, W["w_ple_gate"], W["w_ple"]]
    if final:
        in_specs.append(pl.BlockSpec((1, D_MODEL), lambda t: (0, 0)))
        args.append(W["final_norm"])
    return pl.pallas_call(
        functools.partial(_ple_kernel, final=final, add_tokens=add_tokens),
        grid=(T // tm,),
        in_specs=in_specs,
        out_specs=rows,
        out_shape=jax.ShapeDtypeStruct((T, D_MODEL), F32),
        compiler_params=_params("arbitrary"),
        name="ple_final" if final else "ple",
    )(*args)


def _rope_tables(pos):
    half = RET_HEAD_DIM // 2
    inv = ROPE_THETA ** (-np.arange(half, dtype=np.float64) / half)
    ang = np.asarray(pos, np.float64)[:, None] * inv[None, :]
    cos, sin = np.cos(ang), np.sin(ang)
    reps = LANES // RET_HEAD_DIM
    return (jnp.asarray(np.tile(np.concatenate([cos, cos], axis=-1), (1, reps)), F32),
            jnp.asarray(np.tile(np.concatenate([-sin, sin], axis=-1), (1, reps)), F32))


def _log_decay():
    return np.log1p(-np.power(2.0, -5.0 - np.arange(RET_HEADS, dtype=np.float64)))


def _retention_tables(C):
    log_g = _log_decay()
    idx = np.arange(C, dtype=np.float64)
    rel = idx[:, None] - idx[None, :]
    dmask = np.where(rel >= 0, np.exp(log_g[:, None, None] * np.maximum(rel, 0.0)), 0.0)
    xi = np.exp(log_g[None, :] * (idx[:, None] + 1.0))
    zeta = np.exp(log_g[None, :] * (C - 1.0 - idx)[:, None])
    spread = lambda t: jnp.asarray(np.repeat(t, RET_HEAD_DIM, axis=1), F32)
    return {"dmask": jnp.asarray(dmask, F32), "xi": spread(xi), "zeta": spread(zeta), "decay_c": _chunk_decay(C)}


def _chunk_decay(C):
    return tuple(math.exp(lg * C) for lg in _log_decay())


def _choose_tile(T, cap):
    tm = min(T, cap)
    while T % tm:
        tm //= 2
    return tm


def _layer_prompt(i, h, p, W, ffn_weights, rope, tabs, B, L):
    T = B * L
    tm = _choose_tile(T, 512)
    cb = _choose_tile(L, 256)
    j = i // 2
    z = _in_proj(h, W["norm1"], W["w_in"], i, rope[0], rope[1], _choose_tile(L, tm))
    if i % 2 == 0:
        assert i < DEPTH - 1
        h1, hn2, pool, ret = _mixer_prompt(z, h, tabs, W, i, B, L, cb)
        return _ffn(hn2, h1, None, *ffn_weights, 0, tm, ple=(p, W, i)), pool, ret
    h1, xt, meta, cnt, pool, ret = _mixer_prompt(z, h, tabs, W, i, B, L, cb)
    m_t = _moe_sparse(xt, _moe_plan(meta, cnt, B, L, cb), *ffn_weights, 0, B, L)
    return _ple(h1, m_t, p, W, i, tm), pool, ret


def _layer_sample(i, h, p, W, rope, decay_1, buf, s0):
    B = h.shape[0]
    j = i // 2
    z = _in_proj(h, W["norm1"], W["w_in"], i, rope[0], rope[1], B)
    h1, hn2, gates, pool, ret = _mixer_sample(z, h, buf, s0, decay_1, W, i, _choose_tile(B, 16))
    if i % 2 == 0:
        h2, *ffn_weights = _ffn(hn2, h1, None, W["ffn_g"], W["ffn_u"], W["ffn_d"], j, B, cast_weights=True)
    else:
        h2, *ffn_weights = _ffn(hn2, h1, gates, W["moe_g"], W["moe_u"], W["moe_d"], j, B, cast_weights=True)
    return _ple(h2, None, p, W, i, B), pool, ret, ffn_weights


def _prepare_weights(norm1, w_in, w_pool, pool_scale, w_o, norm2, ffn_w_gate, ffn_w_up, ffn_w_down,
                     moe_router, moe_w_gate, moe_w_up, moe_w_down, w_ple, w_ple_gate, final_norm):
    r_hi, r_lo = _split_bf16(jnp.pad(moe_router, ((0, 0), (0, 0), (0, LANES - N_EXPERTS))))
    rt_hi, rt_lo = _split_bf16(moe_router.transpose(0, 2, 1))
    return {
        "norm1": norm1[:, None, :], "norm2": norm2[:, None, :], "final_norm": final_norm[None, :],
        "pool_scale": pool_scale[:, None, :],
        "w_in": w_in.astype(BF16), "w_pool": w_pool.astype(BF16), "w_o": w_o.astype(BF16),
        "ffn_g": ffn_w_gate, "ffn_u": ffn_w_up, "ffn_d": ffn_w_down,
        "moe_g": moe_w_gate, "moe_u": moe_w_up, "moe_d": moe_w_down,
        "router_hi": r_hi, "router_lo": r_lo, "router_t_hi": rt_hi, "router_t_lo": rt_lo,
        "w_ple": w_ple.astype(BF16), "w_ple_gate": w_ple_gate.astype(BF16),
    }


def _trunk(x_prompt, x_sample, state_pool, state_ret, p_prompt, p_sample, W):
    B, L, _ = x_prompt.shape
    Bs = x_sample.shape[0]
    hp = x_prompt.reshape(B * L, D_MODEL)
    hs = x_sample.reshape(Bs, D_MODEL)
    pp = p_prompt.reshape(DEPTH, B * L, PLE_DIM)
    ps = p_sample.reshape(DEPTH, Bs, PLE_DIM)
    rope_p = _rope_tables(np.arange(L))
    rope_s = _rope_tables(PAST_LEN + np.arange(1))
    tabs = _retention_tables(_choose_tile(L, 256))
    decay_1 = _chunk_decay(1)
    pool_p, ret_p, pool_s, ret_s = [], [], [], []
    for i in range(DEPTH):
        hs, bs, ss, ffn_weights = _layer_sample(i, hs, ps, W, rope_s, decay_1,
                                                state_pool[i].reshape(Bs, POOL_BUF * POOL_WIDTH), state_ret[i])
        hp, bp, sp = _layer_prompt(i, hp, pp, W, ffn_weights, rope_p, tabs, B, L)
        pool_p.append(bp)
        ret_p.append(sp)
        pool_s.append(bs)
        ret_s.append(ss)
    return (hp.reshape(B, L, D_MODEL), hs.reshape(Bs, 1, D_MODEL), jnp.stack(pool_p), jnp.stack(ret_p),
            jnp.stack(pool_s).reshape(DEPTH, Bs, POOL_BUF, POOL_WIDTH), jnp.stack(ret_s))


def kernel(x_prompt, x_sample, state_pool, state_ret, p_prompt, p_sample, norm1, w_in, w_pool, pool_scale, w_o, norm2, ffn_w_gate, ffn_w_up, ffn_w_down, moe_router, moe_w_gate, moe_w_up, moe_w_down, w_ple, w_ple_gate, final_norm):
    W = _prepare_weights(norm1, w_in, w_pool, pool_scale, w_o, norm2, ffn_w_gate, ffn_w_up, ffn_w_down,
                         moe_router, moe_w_gate, moe_w_up, moe_w_down, w_ple, w_ple_gate, final_norm)
    return _trunk(x_prompt, x_sample, state_pool, state_ret, p_prompt, p_sample, W)
```

```python
import functools
import math

import numpy as np

import jax
import jax.numpy as jnp
from jax import lax
from jax.experimental import pallas as pl
from jax.experimental.pallas import tpu as pltpu

F32 = jnp.float32
BF16 = jnp.bfloat16
I32 = jnp.int32

D_MODEL = 1024
DEPTH = 4
PAST_LEN = 16384
POOL_WIDTH = 512
POOL_WINDOWS = (2, 4, 8, 16)
POOL_GROUP_DIM = 128
POOL_BUF = 15
RET_HEADS = 8
RET_HEAD_DIM = 64
RET_WIDTH = 512
ROPE_THETA = 10000.0
IN_WIDTH = POOL_WIDTH + 4 * RET_WIDTH
SECTION = 512
D_FF_TILE = 1408
N_EXPERTS = 8
TOP_K = 2
PLE_DIM = 256
NORM_EPS = 1e-6
LANES = 128
SUBLANES = 8
TOKEN_TILES = D_MODEL // LANES
assert TOKEN_TILES == SUBLANES
TAIL_ROWS = 16
MOE_TILE = 256
MOE_FF_CHUNKS = ((0, 512), (512, 1024), (1024, D_FF_TILE))
VMEM_LIMIT = 58 * 1024 * 1024


def _rms(x, g):
    return x * lax.rsqrt(jnp.mean(x * x, axis=-1, keepdims=True) + NORM_EPS) * g


def _dot(a, b):
    return jnp.dot(a, b, preferred_element_type=F32)


def _dot_nt(a, b):
    return lax.dot_general(a, b, (((1,), (1,)), ((), ())), preferred_element_type=F32)


def _dot_tn(a, b):
    return lax.dot_general(a, b, (((0,), (0,)), ((), ())), preferred_element_type=F32)


def _params(*sem):
    return pltpu.CompilerParams(dimension_semantics=sem, vmem_limit_bytes=VMEM_LIMIT)


def _split_bf16(x):
    hi = x.astype(BF16)
    return hi, (x - hi.astype(F32)).astype(BF16)


def _to_token_layout(ref, x):
    n = x.shape[0]
    for j in range(TOKEN_TILES):
        ref[pl.ds(j, n, stride=TOKEN_TILES), :] = x[:, j * LANES:(j + 1) * LANES]


def _from_token_layout(ref, n):
    return jnp.concatenate([ref[pl.ds(j, n, stride=TOKEN_TILES), :] for j in range(TOKEN_TILES)], axis=1)


def _rope_slab(x, cos, sin, first_half):
    fwd = pltpu.roll(x, 32, 1)
    bwd = pltpu.roll(x, LANES - 32, 1)
    return x * cos + jnp.where(first_half, bwd, fwd) * sin


def _in_proj_kernel(h_ref, n1_ref, w_ref, cos_ref, sin_ref, z_ref, w_out=None):
    if w_out is not None:
        w_out[...] = w_ref[...].astype(BF16)
        w_ref = w_out
    hn = _rms(h_ref[...], n1_ref[...]).astype(BF16)
    cos = cos_ref[...]
    sin = sin_ref[...]
    lane = lax.broadcasted_iota(I32, (hn.shape[0], LANES), 1)
    first_half = (lane & 32) == 0
    for s in range(IN_WIDTH // SECTION):
        zs = _dot(hn, w_ref[:, s * SECTION:(s + 1) * SECTION])
        if s in (1, 2):
            scale = 1.0 if s == 1 else RET_HEAD_DIM ** -0.5
            for c in range(SECTION // LANES):
                slab = _rope_slab(zs[:, c * LANES:(c + 1) * LANES], cos, sin, first_half)
                z_ref[:, s * SECTION + c * LANES:s * SECTION + (c + 1) * LANES] = slab * scale
        else:
            z_ref[:, s * SECTION:(s + 1) * SECTION] = zs


def _in_proj(h, n1, w_in, i, j, cos_t, sin_t, tm, cast_weights=False):
    T = h.shape[0]
    n_tab = cos_t.shape[0] // tm if cos_t.shape[0] > 1 else 1
    tb = tm if cos_t.shape[0] > 1 else 1
    w_block = (None, D_MODEL, IN_WIDTH)
    out_specs = pl.BlockSpec((tm, IN_WIDTH), lambda t: (t, 0))
    out_shape = jax.ShapeDtypeStruct((T, IN_WIDTH), F32)
    if cast_weights:
        assert T == tm, "the weight block must be visited exactly once"
        out_specs = [out_specs, pl.BlockSpec(w_block, lambda t: (0, 0, 0))]
        out_shape = [out_shape, jax.ShapeDtypeStruct((1, D_MODEL, IN_WIDTH), BF16)]
    return pl.pallas_call(
        _in_proj_kernel,
        grid=(T // tm,),
        in_specs=[
            pl.BlockSpec((tm, D_MODEL), lambda t: (t, 0)),
            pl.BlockSpec((None, 1, D_MODEL), lambda t: (i, 0, 0)),
            pl.BlockSpec(w_block, lambda t: (j, 0, 0)),
            pl.BlockSpec((tb, LANES), lambda t: (t % n_tab, 0)),
            pl.BlockSpec((tb, LANES), lambda t: (t % n_tab, 0)),
        ],
        out_specs=out_specs,
        out_shape=out_shape,
        compiler_params=_params("arbitrary"),
        name="in_proj",
    )(h, n1, w_in, cos_t, sin_t)


def _top2(lg, idx, axis, n_idx):
    neg = jnp.float32(-jnp.inf)
    m1 = jnp.max(lg, axis=axis, keepdims=True)
    i1 = jnp.min(jnp.where(lg == m1, idx, n_idx), axis=axis, keepdims=True)
    lg2 = jnp.where(idx == i1, neg, lg)
    m2 = jnp.max(lg2, axis=axis, keepdims=True)
    i2 = jnp.min(jnp.where(lg2 == m2, idx, n_idx), axis=axis, keepdims=True)
    e2 = jnp.exp(m2 - m1)
    den = 1.0 + e2
    return i1, i2, 1.0 / den, e2 / den


def _route(x, rhi_ref, rlo_ref):
    x_hi, x_lo = _split_bf16(x)
    logits = _dot(x_hi, rhi_ref[...]) + _dot(x_lo, rhi_ref[...]) + _dot(x_hi, rlo_ref[...])
    lane = lax.broadcasted_iota(I32, logits.shape, 1)
    lg = jnp.where(lane < N_EXPERTS, logits, jnp.float32(-jnp.inf))
    i1, i2, w1, w2 = _top2(lg, lane, 1, LANES)
    return jnp.where(lane == i1, w1, 0.0) + jnp.where(lane == i2, w2, 0.0)


def _route_logits_t(x, rthi_ref, rtlo_ref):
    x_hi, x_lo = _split_bf16(x)
    return _dot_nt(rthi_ref[...], x_hi) + _dot_nt(rthi_ref[...], x_lo) + _dot_nt(rtlo_ref[...], x_hi)


def _route_ranks(lt, fresh, live, carry_ref, meta_ref, cnt_ref):
    n = lt.shape[1]
    sub = lax.broadcasted_iota(I32, lt.shape, 0)
    i1, i2, w1, w2 = _top2(lt, sub, 0, N_EXPERTS)
    ind = jnp.where(sub == i1, 1.0, 0.0) + jnp.where(sub == i2, 1.0, 0.0)
    tri = jnp.where(lax.broadcasted_iota(I32, (n, n), 0) <= lax.broadcasted_iota(I32, (n, n), 1), 1.0, 0.0)
    csum = _dot(ind.astype(BF16), tri.astype(BF16))
    carry = carry_ref[...] * (1.0 - fresh)
    rank = carry[:, 0:1] + csum - ind
    r1 = jnp.sum(jnp.where(sub == i1, rank, 0.0), axis=0, keepdims=True).astype(I32)
    r2 = jnp.sum(jnp.where(sub == i2, rank, 0.0), axis=0, keepdims=True).astype(I32)
    carry_ref[...] = carry + csum[:, n - 1:n] * live
    cnt_ref[0] = carry_ref[...].astype(I32)
    rows = [i1, i2, r1, r2, pltpu.bitcast(w1, I32), pltpu.bitcast(w2, I32)]
    meta = jnp.zeros(lt.shape, I32)
    for j, rowv in enumerate(rows):
        meta = jnp.where(sub == j, jnp.broadcast_to(rowv, lt.shape), meta)
    meta_ref[0] = meta


def _pool_project(d_slabs, u, wpool_ref, pscale_ref):
    ys = []
    for g in range(len(POOL_WINDOWS)):
        sl = slice(g * POOL_GROUP_DIM, (g + 1) * POOL_GROUP_DIM)
        ys.append(_dot((d_slabs[g] - u[:, sl]).astype(BF16), wpool_ref[g]) * pscale_ref[:, sl])
    return ys


def _mixer_prompt_kernel(*refs, cb, nc, n_chunks, decay_c, moe):
    (u_ref, q_ref, k_ref, v_ref, g_ref, h_ref, dmask_ref, xi_ref, zeta_ref,
     wpool_ref, pscale_ref, wo_ref, n2_ref) = refs[:13]
    if moe:
        (rthi_ref, rtlo_ref, h1_ref, xt_ref, meta_ref, cnt_ref, pool_out_ref, ret_out_ref,
         s_scr, p_scr, mix_scr, carry_scr) = refs[13:]
    else:
        h1_ref, hn2_ref, pool_out_ref, ret_out_ref, s_scr, p_scr, mix_scr = refs[13:]
    step = pl.program_id(0)
    c = jnp.minimum(step, n_chunks - 1) % nc
    c_tail = jnp.maximum(step - 1, 0) % nc

    @pl.when(step == 0)
    def _():
        mix_scr[...] = jnp.zeros_like(mix_scr)
        if moe:
            carry_scr[...] = jnp.zeros_like(carry_scr)

    @pl.when(c == 0)
    def _():
        s_scr[...] = jnp.zeros_like(s_scr)
        p_scr[0:TAIL_ROWS, :] = jnp.zeros((TAIL_ROWS, POOL_WIDTH), F32)

    proj = _dot(mix_scr[...], wo_ref[...])

    u = u_ref[...]
    old_tail = p_scr[0:TAIL_ROWS, :]
    p_scr[TAIL_ROWS:TAIL_ROWS + cb, :] = u
    pos = c * cb + lax.broadcasted_iota(I32, (cb, POOL_GROUP_DIM), 0)
    means = []
    for g, w in enumerate(POOL_WINDOWS):
        sl = slice(g * POOL_GROUP_DIM, (g + 1) * POOL_GROUP_DIM)
        s = u[:, sl]
        for j in range(1, w):
            s = s + p_scr[TAIL_ROWS - j:TAIL_ROWS - j + cb, sl]
        means.append(s / jnp.minimum(pos + 1, w).astype(F32))
    ys = _pool_project(means, u, wpool_ref, pscale_ref)
    new_tail = p_scr[cb:cb + TAIL_ROWS, :]

    h1 = h_ref[...] + proj
    h1_ref[...] = h1
    hn2 = _rms(h1, n2_ref[...])

    q = q_ref[...]
    k = k_ref[...]
    qb = q.astype(BF16)
    kb = k.astype(BF16)
    kz = (k * zeta_ref[...]).astype(BF16)
    vb = v_ref[...].astype(BF16)
    xi = xi_ref[...]
    heads = range(RET_HEADS)
    sls = [slice(hd * RET_HEAD_DIM, (hd + 1) * RET_HEAD_DIM) for hd in heads]
    s_prev = [s_scr[hd] for hd in heads]
    sc = [_dot_nt(qb[:, sl], kb[:, sl]) for sl in sls]
    cross = [_dot(qb[:, sl], s_prev[hd].astype(BF16)) for hd, sl in enumerate(sls)]
    upd = [_dot_tn(kz[:, sl], vb[:, sl]) for sl in sls]

    if moe:
        _to_token_layout(xt_ref, hn2)
        lt = _route_logits_t(hn2, rthi_ref, rtlo_ref)
    else:
        hn2_ref[...] = hn2.astype(BF16)

    pm = [(sc[hd] * dmask_ref[hd]).astype(BF16) for hd in heads]
    o_h = [_dot(pm[hd], vb[:, sl]) + cross[hd] * xi[:, sl] for hd, sl in enumerate(sls)]
    new_s = [s_prev[hd] * decay_c[hd] + upd[hd] for hd in heads]

    if moe:
        _route_ranks(lt, jnp.where(c_tail == 0, 1.0, 0.0), jnp.where(step > 0, 1.0, 0.0), carry_scr, meta_ref, cnt_ref)

    mu = [jnp.mean(o, axis=-1, keepdims=True) for o in o_h]
    cen = [o - m for o, m in zip(o_h, mu)]
    var = [jnp.mean(jnp.square(x), axis=-1, keepdims=True) for x in cen]
    os_ = [x * lax.rsqrt(v + NORM_EPS) for x, v in zip(cen, var)]
    o = jnp.concatenate(os_, axis=1) * jax.nn.silu(g_ref[...])
    mix_scr[...] = jnp.concatenate(ys + [o], axis=1).astype(BF16)

    real = step < n_chunks
    new_tail = jnp.where(real, new_tail, old_tail)
    p_scr[0:TAIL_ROWS, :] = new_tail
    pool_out_ref[0] = new_tail[TAIL_ROWS - POOL_BUF:, :]
    for hd in heads:
        kept = jnp.where(real, new_s[hd], s_prev[hd])
        s_scr[hd] = kept
        ret_out_ref[0, hd] = kept


def _mixer_prompt(z, h, tabs, W, i, B, L, cb):
    T = B * L
    nc = L // cb
    moe = i % 2 == 1
    n_chunks = B * nc
    head = lambda g: jnp.minimum(g, n_chunks - 1)
    tail = lambda g: jnp.maximum(g - 1, 0)
    zspec = lambda s: pl.BlockSpec((cb, SECTION), lambda g, s=s: (head(g), s))
    const2 = lambda shape: pl.BlockSpec(shape, lambda g: (0, 0))
    const3 = lambda shape: pl.BlockSpec(shape, lambda g: (0, 0, 0))
    layer3 = lambda shape: pl.BlockSpec((None,) + shape, lambda g: (i, 0, 0))
    rows = pl.BlockSpec((cb, D_MODEL), lambda g: (tail(g), 0))
    in_specs = [zspec(0), zspec(1), zspec(2), zspec(3), zspec(4), rows,
                const3((RET_HEADS, cb, cb)), const2((cb, RET_WIDTH)), const2((cb, RET_WIDTH)),
                pl.BlockSpec((None, len(POOL_WINDOWS), POOL_GROUP_DIM, POOL_GROUP_DIM), lambda g: (i, 0, 0, 0)),
                layer3((1, POOL_WIDTH)), layer3((D_MODEL, D_MODEL)), layer3((1, D_MODEL))]
    args = [z, z, z, z, z, h, tabs["dmask"], tabs["xi"], tabs["zeta"],
            W["w_pool"], W["pool_scale"], W["w_o"], W["norm2"]]
    scratch = [pltpu.VMEM((RET_HEADS, RET_HEAD_DIM, RET_HEAD_DIM), F32),
               pltpu.VMEM((TAIL_ROWS + cb, POOL_WIDTH), F32),
               pltpu.VMEM((cb, D_MODEL), BF16)]
    state_specs = [pl.BlockSpec((1, POOL_BUF, POOL_WIDTH), lambda g: (head(g) // nc, 0, 0)),
                   pl.BlockSpec((1, RET_HEADS, RET_HEAD_DIM, RET_HEAD_DIM), lambda g: (head(g) // nc, 0, 0, 0))]
    state_shapes = [jax.ShapeDtypeStruct((B, POOL_BUF, POOL_WIDTH), F32),
                    jax.ShapeDtypeStruct((B, RET_HEADS, RET_HEAD_DIM, RET_HEAD_DIM), F32)]
    if moe:
        j = i // 2
        rt = pl.BlockSpec((None, N_EXPERTS, D_MODEL), lambda g: (j, 0, 0))
        in_specs += [rt, rt]
        args += [W["router_t_hi"], W["router_t_lo"]]
        out_specs = [rows,
                     pl.BlockSpec((cb * TOKEN_TILES, LANES), lambda g: (tail(g), 0)),
                     pl.BlockSpec((1, SUBLANES, cb), lambda g: (tail(g), 0, 0)),
                     pl.BlockSpec((1, N_EXPERTS, LANES), lambda g: (tail(g) // nc, 0, 0))]
        out_shape = [jax.ShapeDtypeStruct((T, D_MODEL), F32),
                     jax.ShapeDtypeStruct((T * TOKEN_TILES, LANES), F32),
                     jax.ShapeDtypeStruct((B * nc, SUBLANES, cb), I32),
                     jax.ShapeDtypeStruct((B, N_EXPERTS, LANES), I32)]
        scratch.append(pltpu.VMEM((N_EXPERTS, LANES), F32))
    else:
        out_specs = [rows, rows]
        out_shape = [jax.ShapeDtypeStruct((T, D_MODEL), F32), jax.ShapeDtypeStruct((T, D_MODEL), BF16)]
    return pl.pallas_call(
        functools.partial(_mixer_prompt_kernel, cb=cb, nc=nc, n_chunks=n_chunks, decay_c=tabs["decay_c"], moe=moe),
        grid=(n_chunks + 1,),
        in_specs=in_specs,
        out_specs=out_specs + state_specs,
        out_shape=out_shape + state_shapes,
        scratch_shapes=scratch,
        compiler_params=_params("arbitrary"),
        name="mixer_prompt",
    )(*args)


def _to_head_rows(ref, x):
    n = x.shape[0]
    for hd in range(RET_HEADS):
        ref[pl.ds(hd, n, stride=RET_HEADS), 0:RET_HEAD_DIM] = x[:, hd * RET_HEAD_DIM:(hd + 1) * RET_HEAD_DIM]


def _mixer_sample_kernel(*refs, bb, decay_1, moe):
    (u_ref, q_ref, k_ref, v_ref, g_ref, h_ref, buf_ref, s_ref,
     wpool_ref, pscale_ref, wo_ref, n2_ref) = refs[:12]
    n_in = 14 if moe else 12
    router_refs = refs[12:n_in]
    h1_ref, hn2_ref = refs[n_in:n_in + 2]
    gates_ref = refs[n_in + 2] if moe else None
    buf_out_ref, s_out_ref, q2_scr, k2_scr, v2_scr, g2_scr, c2_scr, o2_scr, qrep_scr, krep_scr = refs[-10:]
    dh = RET_HEAD_DIM
    grp = SUBLANES

    u = u_ref[...]
    means = []
    for g, w in enumerate(POOL_WINDOWS):
        s = u[:, g * POOL_GROUP_DIM:(g + 1) * POOL_GROUP_DIM]
        for j in range(1, w):
            lo = (POOL_BUF - j) * POOL_WIDTH + g * POOL_GROUP_DIM
            s = s + buf_ref[:, lo:lo + POOL_GROUP_DIM]
        means.append(s / float(min(PAST_LEN + 1, w)))
    ys = _pool_project(means, u, wpool_ref, pscale_ref)
    keep = (POOL_BUF - 1) * POOL_WIDTH
    buf_out_ref[:, 0:keep] = buf_ref[:, POOL_WIDTH:POOL_WIDTH + keep]
    buf_out_ref[:, keep:keep + POOL_WIDTH] = u

    for ref, src in ((q2_scr, q_ref), (k2_scr, k_ref), (v2_scr, v_ref), (g2_scr, g_ref)):
        _to_head_rows(ref, src[...])
    spread = jnp.where(lax.broadcasted_iota(I32, (grp, grp * LANES), 0)
                       == lax.broadcasted_iota(I32, (grp, grp * LANES), 1) // LANES, 1.0, 0.0)

    def per_group(gi, carry):
        rows = pl.ds(pl.multiple_of(gi * grp, grp), grp)
        qrep_scr[...] = _dot_tn(q_ref[rows, :], spread)
        krep_scr[...] = _dot_tn(k_ref[rows, :], spread)
        for bl in range(grp):
            b = gi * grp + bl
            cross = []
            for hd in range(RET_HEADS):
                tile = (slice(hd * dh, (hd + 1) * dh), slice(bl * LANES, bl * LANES + dh))
                s_prev = s_ref[b, hd]
                cross.append(jnp.sum(s_prev * qrep_scr[tile], axis=0, keepdims=True))
                v_row = v2_scr[pl.ds(b * RET_HEADS + hd, 1), 0:dh]
                s_out_ref[b, hd] = s_prev * decay_1[hd] + krep_scr[tile] * v_row
            c2_scr[pl.ds(pl.multiple_of(b * RET_HEADS, RET_HEADS), RET_HEADS), 0:dh] = jnp.concatenate(cross, axis=0)
        return carry

    lax.fori_loop(0, bb // grp, per_group, 0)

    n2 = bb * RET_HEADS
    q2, k2, v2 = q2_scr[:, 0:dh], k2_scr[:, 0:dh], v2_scr[:, 0:dh]
    head = lax.broadcasted_iota(I32, (n2, dh), 0) % RET_HEADS
    gamma = jnp.zeros((n2, dh), F32)
    for hd in range(RET_HEADS):
        gamma = jnp.where(head == hd, decay_1[hd], gamma)
    o2 = jnp.sum(q2 * k2, axis=-1, keepdims=True) * v2 + gamma * c2_scr[:, 0:dh]
    mu = jnp.mean(o2, axis=-1, keepdims=True)
    var = jnp.mean(jnp.square(o2 - mu), axis=-1, keepdims=True)
    o2_scr[:, 0:dh] = (o2 - mu) * lax.rsqrt(var + NORM_EPS) * jax.nn.silu(g2_scr[:, 0:dh])
    o = [o2_scr[pl.ds(hd, bb, stride=RET_HEADS), 0:dh] for hd in range(RET_HEADS)]
    mix = jnp.concatenate(ys + o, axis=1)
    h1 = h_ref[...] + _dot(mix.astype(BF16), wo_ref[...])
    h1_ref[...] = h1
    hn2 = _rms(h1, n2_ref[...])
    hn2_ref[...] = hn2.astype(BF16)
    if moe:
        gates_ref[...] = _route(hn2, *router_refs)


def _mixer_sample(z, h, buf, s0, decay_1, W, i, bb):
    B = h.shape[0]
    moe = i % 2 == 1
    flat = POOL_BUF * POOL_WIDTH
    zspec = lambda s: pl.BlockSpec((bb, SECTION), lambda t, s=s: (t, s))
    layer3 = lambda shape: pl.BlockSpec((None,) + shape, lambda t: (i, 0, 0))
    rows = pl.BlockSpec((bb, D_MODEL), lambda t: (t, 0))
    pool = pl.BlockSpec((bb, flat), lambda t: (t, 0))
    state = pl.BlockSpec((bb, RET_HEADS, RET_HEAD_DIM, RET_HEAD_DIM), lambda t: (t, 0, 0, 0))
    in_specs = [zspec(0), zspec(1), zspec(2), zspec(3), zspec(4), rows, pool, state,
                pl.BlockSpec((None, len(POOL_WINDOWS), POOL_GROUP_DIM, POOL_GROUP_DIM), lambda t: (i, 0, 0, 0)),
                layer3((1, POOL_WIDTH)), layer3((D_MODEL, D_MODEL)), layer3((1, D_MODEL))]
    args = [z, z, z, z, z, h, buf, s0, W["w_pool"], W["pool_scale"], W["w_o"], W["norm2"]]
    out_specs = [rows, rows]
    out_shape = [jax.ShapeDtypeStruct((B, D_MODEL), F32), jax.ShapeDtypeStruct((B, D_MODEL), BF16)]
    if moe:
        j = i // 2
        r = pl.BlockSpec((None, D_MODEL, LANES), lambda t: (j, 0, 0))
        in_specs += [r, r]
        args += [W["router_hi"], W["router_lo"]]
        out_specs.append(pl.BlockSpec((bb, LANES), lambda t: (t, 0)))
        out_shape.append(jax.ShapeDtypeStruct((B, LANES), F32))
    out_specs += [pool, state]
    out_shape += [jax.ShapeDtypeStruct((B, flat), F32), jax.ShapeDtypeStruct(s0.shape, F32)]
    outs = pl.pallas_call(
        functools.partial(_mixer_sample_kernel, bb=bb, decay_1=decay_1, moe=moe),
        grid=(B // bb,),
        in_specs=in_specs,
        out_specs=out_specs,
        out_shape=out_shape,
        scratch_shapes=[pltpu.VMEM((bb * RET_HEADS, LANES), F32) for _ in range(6)]
        + [pltpu.VMEM((RET_WIDTH, SUBLANES * LANES), F32) for _ in range(2)],
        compiler_params=_params("arbitrary"),
        name="mixer_sample",
    )(*args)
    if moe:
        h1, hn2, gates, pool, ret = outs
    else:
        (h1, hn2, pool, ret), gates = outs, None
    return h1, hn2, gates, pool, ret


def _swiglu(x, wg_ref, wu_ref, wd_ref, between_chunks=None):
    y = None
    for c, (lo, hi) in enumerate(MOE_FF_CHUNKS):
        if between_chunks is not None:
            between_chunks(c)
        hidden = jax.nn.silu(_dot(x, wg_ref[:, lo:hi])) * _dot(x, wu_ref[:, lo:hi])
        part = _dot(hidden.astype(BF16), wd_ref[lo:hi, :])
        y = part if y is None else y + part
    return y


def _ple_update(h, p_ref, wg_ref, wp_ref):
    gate = jax.nn.sigmoid(_dot(h.astype(BF16), wg_ref[...]))
    return h + gate * _dot(p_ref[...].astype(BF16), wp_ref[...])


def _ffn_kernel(*refs, moe, ple, cast_weights):
    refs = list(refs)
    x_ref, h_ref = refs[:2]
    del refs[:2]
    gates_ref = refs.pop(0) if moe else None
    wg_ref, wu_ref, wd_ref = refs[:3]
    ple_refs = refs[3:6] if ple else None
    acc_ref = refs[-1]
    if cast_weights:
        out_ref, wg_out, wu_out, wd_out = refs[-5:-1]
        for src, dst in ((wg_ref, wg_out), (wu_ref, wu_out), (wd_ref, wd_out)):
            dst[...] = src[...].astype(BF16)
        wg_ref, wu_ref, wd_ref = wg_out, wu_out, wd_out
    else:
        out_ref = refs[-2]
    f = pl.program_id(1)
    y = _swiglu(x_ref[...], wg_ref, wu_ref, wd_ref)
    if moe:
        gates = gates_ref[...]
        lane = lax.broadcasted_iota(I32, gates.shape, 1)
        y = y * jnp.sum(jnp.where(lane == f, gates, 0.0), axis=-1, keepdims=True)

    @pl.when(f == 0)
    def _():
        acc_ref[...] = y

    @pl.when(f > 0)
    def _():
        acc_ref[...] += y

    @pl.when(f == pl.num_programs(1) - 1)
    def _():
        h2 = h_ref[...] + acc_ref[...]
        out_ref[...] = _ple_update(h2, *ple_refs) if ple else h2


def _ffn(x, h, gates, wg, wu, wd, j, tm, ple=None, cast_weights=False):
    T = x.shape[0]
    moe = gates is not None
    tf = D_FF_TILE
    row = pl.BlockSpec((tm, D_MODEL), lambda t, f: (t, 0))
    in_specs = [row, row]
    args = [x, h]
    if moe:
        nf = wg.shape[1]
        in_specs.append(pl.BlockSpec((tm, LANES), lambda t, f: (t, 0)))
        args.append(gates)
        w_blocks = [((None, None, D_MODEL, tf), lambda jj: lambda t, f: (jj, f, 0, 0))] * 2
        w_blocks.append(((None, None, tf, D_MODEL), lambda jj: lambda t, f: (jj, f, 0, 0)))
    else:
        nf = wg.shape[2] // tf
        w_blocks = [((None, D_MODEL, tf), lambda jj: lambda t, f: (jj, 0, f))] * 2
        w_blocks.append(((None, tf, D_MODEL), lambda jj: lambda t, f: (jj, f, 0)))
    in_specs += [pl.BlockSpec(shape, index(j)) for shape, index in w_blocks]
    args += [wg, wu, wd]
    if ple is not None:
        p, W, i = ple
        in_specs += [pl.BlockSpec((None, tm, PLE_DIM), lambda t, f: (i, t, 0)),
                     pl.BlockSpec((None, D_MODEL, D_MODEL), lambda t, f: (i, 0, 0)),
                     pl.BlockSpec((None, PLE_DIM, D_MODEL), lambda t, f: (i, 0, 0))]
        args += [p, W["w_ple_gate"], W["w_ple"]]
    out_specs = row
    out_shape = jax.ShapeDtypeStruct((T, D_MODEL), F32)
    if cast_weights:
        assert T == tm, "each weight block must be visited exactly once"
        out_specs = [row] + [pl.BlockSpec(shape, index(0)) for shape, index in w_blocks]
        out_shape = [out_shape] + [jax.ShapeDtypeStruct((1,) + w.shape[1:], BF16) for w in (wg, wu, wd)]
    return pl.pallas_call(
        functools.partial(_ffn_kernel, moe=moe, ple=ple is not None, cast_weights=cast_weights),
        grid=(T // tm, nf),
        in_specs=in_specs,
        out_specs=out_specs,
        out_shape=out_shape,
        scratch_shapes=[pltpu.VMEM((tm, D_MODEL), F32)],
        compiler_params=_params("arbitrary", "arbitrary"),
        name="ffn_moe" if moe else "ffn_dense",
    )(*args)


def _moe_sparse_kernel(dest_ref, w1_ref, w2_ref, tile_e_ref, tile_rows_ref, n_valid_ref,
                       xt_ref, wg_ref, wu_ref, wd_ref, out_ref,
                       src_ref, ws_ref, g_ref, y_ref, *, tb):
    b = pl.program_id(0)
    j = pl.program_id(1)
    n_steps = pl.num_programs(1)
    n_valid = n_valid_ref[b]
    ts = MOE_TILE
    unroll = SUBLANES
    tile8 = lambda r: pl.ds(pl.multiple_of(r * TOKEN_TILES, TOKEN_TILES), TOKEN_TILES)
    rows8 = lambda start: pl.ds(pl.multiple_of(start, TOKEN_TILES), TOKEN_TILES)
    slot_rows = tb * TOKEN_TILES

    def gather_row(tile, buf, r):
        g_ref[buf, tile8(r), :] = xt_ref[rows8(src_ref[tile * ts + r] & (slot_rows - 1)), :]

    def scatter_row(tile, buf, r):
        out_ref[rows8(src_ref[tile * ts + r]), :] = ws_ref[tile * ts + r] * y_ref[buf, tile8(r), :]

    def row_loop(row_fn, tile, buf):
        def group(g, carry):
            for u in range(unroll):
                row_fn(tile, buf, g * unroll + u)
            return carry
        lax.fori_loop(0, ts // unroll, group, 0)

    @pl.when((b == 0) & (j == 0))
    def _():
        g_ref[...] = jnp.zeros_like(g_ref)
        y_ref[...] = jnp.zeros_like(y_ref)

    @pl.when(j == 0)
    def _():
        def fill(t, carry):
            d = dest_ref[b * tb + t]
            d1 = d & 0xFFFF
            d2 = d >> 16
            src_ref[d1] = t * TOKEN_TILES
            src_ref[d2] = t * TOKEN_TILES + slot_rows
            ws_ref[d1] = w1_ref[b * tb + t]
            ws_ref[d2] = w2_ref[b * tb + t]
            return carry

        lax.fori_loop(0, tb, fill, 0, unroll=unroll)

        def pad_tile(tile, carry):
            n_rows = tile_rows_ref[b * n_steps + tile]
            last = tile * ts + n_rows - 1

            def pad_row(r, c):
                src_ref[tile * ts + r] = src_ref[last]
                ws_ref[tile * ts + r] = ws_ref[last]
                return c

            return lax.fori_loop(n_rows, ts, pad_row, carry)

        lax.fori_loop(0, n_valid, pad_tile, 0)
        row_loop(gather_row, 0, 0)

    @pl.when(j < n_valid)
    def _():
        cur = j & 1
        nxt = 1 - cur
        tile_next = jnp.minimum(j + 1, n_valid - 1)
        tile_prev = jnp.maximum(j - 1, 0)
        x = _from_token_layout(g_ref.at[cur], ts).astype(BF16)
        n_chunks = len(MOE_FF_CHUNKS)

        def move_rows(c):
            for r in range(c * ts // n_chunks, (c + 1) * ts // n_chunks):
                gather_row(tile_next, nxt, r)
                scatter_row(tile_prev, nxt, r)

        _to_token_layout(y_ref.at[cur], _swiglu(x, wg_ref, wu_ref, wd_ref, move_rows))

    @pl.when(j == n_valid)
    def _():
        row_loop(scatter_row, n_valid - 1, (n_valid - 1) & 1)


def _moe_plan(meta, cnt, B, L, cb):
    ts = MOE_TILE
    n_steps = TOP_K * L // ts + N_EXPERTS + 1
    m = meta.reshape(B, L // cb, SUBLANES, cb).transpose(2, 0, 1, 3).reshape(SUBLANES, B, L)
    e1, e2, r1, r2 = m[0], m[1], m[2], m[3]
    w1 = lax.bitcast_convert_type(m[4], F32).reshape(B * L)
    w2 = lax.bitcast_convert_type(m[5], F32).reshape(B * L)
    cnt = cnt[:, :, 0]
    padded = (cnt + ts - 1) // ts * ts
    ends = jnp.cumsum(padded, axis=1)
    off = ends - padded
    experts = jnp.arange(N_EXPERTS, dtype=I32)

    def pick(table, idx):
        return jnp.sum(jnp.where(idx[:, :, None] == experts, table[:, None, :], 0), axis=2)

    d1 = pick(off, e1) + r1
    d2 = pick(off, e2) + r2
    dest = (d1 | (d2 << 16)).reshape(B * L).astype(I32)
    start = jnp.arange(n_steps, dtype=I32) * ts
    tile_e = jnp.sum((ends[:, None, :] <= start[None, :, None]).astype(I32), axis=2)
    valid = start[None, :] < ends[:, -1:]
    last_e = jnp.max(jnp.where(valid, tile_e, 0), axis=1, keepdims=True)
    tile_e = jnp.where(valid, tile_e, last_e)
    filled = pick(cnt, tile_e) - (start[None, :] - pick(off, tile_e))
    tile_rows = jnp.where(valid, jnp.clip(filled, 0, ts), 0)
    n_valid = (ends[:, -1] // ts).astype(I32)
    return dest, w1, w2, tile_e.reshape(-1).astype(I32), tile_rows.reshape(-1).astype(I32), n_valid, n_steps


def _moe_sparse(xt, plan, wg, wu, wd, j, B, L):
    dest, w1, w2, tile_e, tile_rows, n_valid, n_steps = plan
    ts = MOE_TILE
    tf = D_FF_TILE
    slot_rows = L * TOKEN_TILES
    assert slot_rows & (slot_rows - 1) == 0, "the kernel masks row offsets with slot_rows - 1"
    tok = pl.BlockSpec((slot_rows, LANES), lambda b, k, *_: (b, 0), pipeline_mode=pl.Buffered(1))
    tok_out = pl.BlockSpec((TOP_K * slot_rows, LANES), lambda b, k, *_: (b, 0), pipeline_mode=pl.Buffered(1))
    expert = lambda b, k, dest, w1, w2, te, tr, nv: (j, te[b * n_steps + k], 0, 0)
    grid_spec = pltpu.PrefetchScalarGridSpec(
        num_scalar_prefetch=6,
        grid=(B, n_steps),
        in_specs=[tok,
                  pl.BlockSpec((None, None, D_MODEL, tf), expert),
                  pl.BlockSpec((None, None, D_MODEL, tf), expert),
                  pl.BlockSpec((None, None, tf, D_MODEL), expert)],
        out_specs=tok_out,
        scratch_shapes=[pltpu.SMEM((n_steps * ts,), I32), pltpu.SMEM((n_steps * ts,), F32),
                        pltpu.VMEM((2, ts * TOKEN_TILES, LANES), F32), pltpu.VMEM((2, ts * TOKEN_TILES, LANES), F32)],
    )
    out = pl.pallas_call(
        functools.partial(_moe_sparse_kernel, tb=L),
        grid_spec=grid_spec,
        out_shape=jax.ShapeDtypeStruct((B * TOP_K * slot_rows, LANES), F32),
        compiler_params=_params("arbitrary", "arbitrary"),
        name="moe_sparse",
    )(dest, w1, w2, tile_e, tile_rows, n_valid, xt, wg, wu, wd)
    return out.reshape(B, TOP_K, slot_rows, LANES)


def _ple_kernel(*refs, final, add_tokens):
    refs = list(refs)
    h_ref = refs.pop(0)
    m_ref = refs.pop(0) if add_tokens else None
    p_ref, wg_ref, wp_ref = refs[:3]
    fn_ref = refs[3] if final else None
    out_ref = refs[-1]
    h = h_ref[...]
    if add_tokens:
        for s in range(TOP_K):
            h = h + _from_token_layout(m_ref.at[s], h.shape[0])
    h3 = _ple_update(h, p_ref, wg_ref, wp_ref)
    out_ref[...] = _rms(h3, fn_ref[...]) if final else h3


def _ple(h, m_t, p, W, i, tm):
    T = h.shape[0]
    final = i == DEPTH - 1
    add_tokens = m_t is not None
    rows = pl.BlockSpec((tm, D_MODEL), lambda t: (t, 0))
    in_specs = [rows]
    args = [h]
    if add_tokens:
        per_seq = m_t.shape[2] // (tm * TOKEN_TILES)
        in_specs.append(pl.BlockSpec((None, TOP_K, tm * TOKEN_TILES, LANES),
                                     lambda t: (t // per_seq, 0, t % per_seq, 0)))
        args.append(m_t)
    in_specs += [pl.BlockSpec((None, tm, PLE_DIM), lambda t: (i, t, 0)),
                 pl.BlockSpec((None, D_MODEL, D_MODEL), lambda t: (i, 0, 0)),
                 pl.BlockSpec((None, PLE_DIM, D_MODEL), lambda t: (i, 0, 0))]
    args += [p, W["w_ple_gate"], W["w_ple"]]
    if final:
        in_specs.append(pl.BlockSpec((1, D_MODEL), lambda t: (0, 0)))
        args.append(W["final_norm"])
    return pl.pallas_call(
        functools.partial(_ple_kernel, final=final, add_tokens=add_tokens),
        grid=(T // tm,),
        in_specs=in_specs,
        out_specs=rows,
        out_shape=jax.ShapeDtypeStruct((T, D_MODEL), F32),
        compiler_params=_params("arbitrary"),
        name="ple_final" if final else "ple",
    )(*args)


def _rope_tables(pos):
    half = RET_HEAD_DIM // 2
    inv = ROPE_THETA ** (-np.arange(half, dtype=np.float64) / half)
    ang = np.asarray(pos, np.float64)[:, None] * inv[None, :]
    cos, sin = np.cos(ang), np.sin(ang)
    reps = LANES // RET_HEAD_DIM
    return (jnp.asarray(np.tile(np.concatenate([cos, cos], axis=-1), (1, reps)), F32),
            jnp.asarray(np.tile(np.concatenate([-sin, sin], axis=-1), (1, reps)), F32))


def _log_decay():
    return np.log1p(-np.power(2.0, -5.0 - np.arange(RET_HEADS, dtype=np.float64)))


def _retention_tables(C):
    log_g = _log_decay()
    idx = np.arange(C, dtype=np.float64)
    rel = idx[:, None] - idx[None, :]
    dmask = np.where(rel >= 0, np.exp(log_g[:, None, None] * np.maximum(rel, 0.0)), 0.0)
    xi = np.exp(log_g[None, :] * (idx[:, None] + 1.0))
    zeta = np.exp(log_g[None, :] * (C - 1.0 - idx)[:, None])
    spread = lambda t: jnp.asarray(np.repeat(t, RET_HEAD_DIM, axis=1), F32)
    return {"dmask": jnp.asarray(dmask, F32), "xi": spread(xi), "zeta": spread(zeta), "decay_c": _chunk_decay(C)}


def _chunk_decay(C):
    return tuple(math.exp(lg * C) for lg in _log_decay())


def _choose_tile(T, cap):
    tm = min(T, cap)
    while T % tm:
        tm //= 2
    return tm


def _layer_prompt(i, h, p, W, w_in, ffn_weights, rope, tabs, B, L):
    T = B * L
    tm = _choose_tile(T, 512)
    cb = _choose_tile(L, 256)
    z = _in_proj(h, W["norm1"], w_in, i, 0, rope[0], rope[1], _choose_tile(L, tm))
    if i % 2 == 0:
        assert i < DEPTH - 1
        h1, hn2, pool, ret = _mixer_prompt(z, h, tabs, W, i, B, L, cb)
        return _ffn(hn2, h1, None, *ffn_weights, 0, tm, ple=(p, W, i)), pool, ret
    h1, xt, meta, cnt, pool, ret = _mixer_prompt(z, h, tabs, W, i, B, L, cb)
    m_t = _moe_sparse(xt, _moe_plan(meta, cnt, B, L, cb), *ffn_weights, 0, B, L)
    return _ple(h1, m_t, p, W, i, tm), pool, ret


def _layer_sample(i, h, p, W, rope, decay_1, buf, s0):
    B = h.shape[0]
    j = i // 2
    z, w_in = _in_proj(h, W["norm1"], W["w_in"], i, i, rope[0], rope[1], B, cast_weights=True)
    h1, hn2, gates, pool, ret = _mixer_sample(z, h, buf, s0, decay_1, W, i, _choose_tile(B, 16))
    if i % 2 == 0:
        h2, *ffn_weights = _ffn(hn2, h1, None, W["ffn_g"], W["ffn_u"], W["ffn_d"], j, B, cast_weights=True)
    else:
        h2, *ffn_weights = _ffn(hn2, h1, gates, W["moe_g"], W["moe_u"], W["moe_d"], j, B, cast_weights=True)
    return _ple(h2, None, p, W, i, B), pool, ret, w_in, ffn_weights


def _prepare_weights(norm1, w_in, w_pool, pool_scale, w_o, norm2, ffn_w_gate, ffn_w_up, ffn_w_down,
                     moe_router, moe_w_gate, moe_w_up, moe_w_down, w_ple, w_ple_gate, final_norm):
    r_hi, r_lo = _split_bf16(jnp.pad(moe_router, ((0, 0), (0, 0), (0, LANES - N_EXPERTS))))
    rt_hi, rt_lo = _split_bf16(moe_router.transpose(0, 2, 1))
    return {
        "norm1": norm1[:, None, :], "norm2": norm2[:, None, :], "final_norm": final_norm[None, :],
        "pool_scale": pool_scale[:, None, :],
        "w_in": w_in,
        "w_pool": w_pool.astype(BF16), "w_o": w_o.astype(BF16),
        "ffn_g": ffn_w_gate, "ffn_u": ffn_w_up, "ffn_d": ffn_w_down,
        "moe_g": moe_w_gate, "moe_u": moe_w_up, "moe_d": moe_w_down,
        "router_hi": r_hi, "router_lo": r_lo, "router_t_hi": rt_hi, "router_t_lo": rt_lo,
        "w_ple": w_ple.astype(BF16), "w_ple_gate": w_ple_gate.astype(BF16),
    }


def _trunk(x_prompt, x_sample, state_pool, state_ret, p_prompt, p_sample, W):
    B, L, _ = x_prompt.shape
    Bs = x_sample.shape[0]
    hp = x_prompt.reshape(B * L, D_MODEL)
    hs = x_sample.reshape(Bs, D_MODEL)
    pp = p_prompt.reshape(DEPTH, B * L, PLE_DIM)
    ps = p_sample.reshape(DEPTH, Bs, PLE_DIM)
    rope_p = _rope_tables(np.arange(L))
    rope_s = _rope_tables(PAST_LEN + np.arange(1))
    tabs = _retention_tables(_choose_tile(L, 256))
    decay_1 = _chunk_decay(1)
    pool_p, ret_p, pool_s, ret_s = [], [], [], []
    for i in range(DEPTH):
        hs, bs, ss, w_in, ffn_weights = _layer_sample(i, hs, ps, W, rope_s, decay_1,
                                                      state_pool[i].reshape(Bs, POOL_BUF * POOL_WIDTH), state_ret[i])
        hp, bp, sp = _layer_prompt(i, hp, pp, W, w_in, ffn_weights, rope_p, tabs, B, L)
        pool_p.append(bp)
        ret_p.append(sp)
        pool_s.append(bs)
        ret_s.append(ss)
    return (hp.reshape(B, L, D_MODEL), hs.reshape(Bs, 1, D_MODEL), jnp.stack(pool_p), jnp.stack(ret_p),
            jnp.stack(pool_s).reshape(DEPTH, Bs, POOL_BUF, POOL_WIDTH), jnp.stack(ret_s))


def kernel(x_prompt, x_sample, state_pool, state_ret, p_prompt, p_sample, norm1, w_in, w_pool, pool_scale, w_o, norm2, ffn_w_gate, ffn_w_up, ffn_w_down, moe_router, moe_w_gate, moe_w_up, moe_w_down, w_ple, w_ple_gate, final_norm):
    W = _prepare_weights(norm1, w_in, w_pool, pool_scale, w_o, norm2, ffn_w_gate, ffn_w_up, ffn_w_down,
                         moe_router, moe_w_gate, moe_w_up, moe_w_down, w_ple, w_ple_gate, final_norm)
    return _trunk(x_prompt, x_sample, state_pool, state_ret, p_prompt, p_sample, W)
```
